```python
import math
import jax, jax.numpy as jnp
from jax import lax
import numpy as np

D_MODEL = 1024
BATCH = 16
SEQ = 2048
DEPTH = 4

A_HEADS = 4
A_HEAD_DIM = 128
IDX_HEADS = 8
IDX_DIM = 64
TOPK_MAX = 256
Q_BLOCK = 128
B_HEADS = 4
B_HEAD_DIM = 128
CONV_WIDTH = 4
GDN_CHUNK = 64
C_HEADS = 4
C_KEY_DIM = 64
C_VAL_DIM = 128
RET_CHUNK = 64
D_FF = 4 * D_MODEL
ROPE_THETA = 10000.0
EPS = 1e-6
N_BRANCH = 3
A_W = A_HEADS * A_HEAD_DIM
B_W = B_HEADS * B_HEAD_DIM
C_W = C_HEADS * C_VAL_DIM
C_QK = C_HEADS * C_KEY_DIM
IN_WIDTHS = (A_W, A_HEAD_DIM, A_HEAD_DIM, IDX_HEADS * IDX_DIM, IDX_DIM, IDX_HEADS,
             B_W, B_W, B_W, B_HEADS, B_HEADS, B_W,
             C_QK, C_QK, C_W, C_W,
             N_BRANCH * D_MODEL)
IN_WIDTH = sum(IN_WIDTHS)

kernel_name = "hybrid_dsa_gdn_retention_gated_merge"


def rms_norm(x, gain=None):
    xf = x.astype(jnp.float32)
    y = xf * lax.rsqrt(jnp.mean(xf * xf, axis=-1, keepdims=True) + EPS)
    if gain is not None:
        y = y * gain.astype(jnp.float32)
    return y.astype(x.dtype)


def l2_norm(x):
    xf = x.astype(jnp.float32)
    return xf * lax.rsqrt(jnp.sum(xf * xf, axis=-1, keepdims=True) + EPS)


def rope_angles(seq, dim):
    pos = jnp.arange(seq, dtype=jnp.float32)
    inv = ROPE_THETA ** (-jnp.arange(0, dim, 2, dtype=jnp.float32) / dim)
    return pos[:, None] * inv[None, :]


def retnet_angles(seq, dim):
    pos = jnp.arange(seq, dtype=jnp.float32)
    inv = 1.0 / (ROPE_THETA ** jnp.linspace(0.0, 1.0, dim // 2, dtype=jnp.float32))
    return pos[:, None] * inv[None, :]


def rotate(x, angles):
    half = x.shape[-1] // 2
    cos = jnp.cos(angles)[None, :, None, :].astype(x.dtype)
    sin = jnp.sin(angles)[None, :, None, :].astype(x.dtype)
    x1, x2 = x[..., :half], x[..., half:]
    return jnp.concatenate([x1 * cos - x2 * sin, x2 * cos + x1 * sin], axis=-1)


def split_projection(p):
    offsets = [int(o) for o in np.cumsum(np.array(IN_WIDTHS))[:-1]]
    return jnp.split(p, offsets, axis=-1)


def to_chunks(a, chunk):
    bsz, seq = a.shape[:2]
    a = a.astype(jnp.float32).reshape((bsz, seq // chunk, chunk) + a.shape[2:])
    return jnp.moveaxis(a, 3, 1)


def dsa_attention(q, k, v, iq, ik, iw):
    bsz, seq = q.shape[:2]
    n_sel = min(TOPK_MAX, seq // 4)
    nb = seq // Q_BLOCK
    f32 = jnp.float32
    ik32 = ik.astype(f32)
    kpos = jnp.arange(seq)

    def blocks(a):
        return jnp.swapaxes(a.reshape((bsz, nb, Q_BLOCK) + a.shape[2:]), 0, 1)

    def one_block(args):
        qb, iqb, iwb, pb = args
        logits = jnp.einsum('bthd,bsd->bths', iqb.astype(f32), ik32) * (IDX_DIM ** -0.5)
        scores = jnp.einsum('bths,bth->bts', jax.nn.relu(logits), iwb.astype(f32) * (IDX_HEADS ** -0.5))
        causal = kpos[None, :] <= pb[:, None]
        scores = jnp.where(causal[None], scores, -jnp.inf)
        _, idx = lax.top_k(scores, n_sel)
        valid = idx <= pb[None, :, None]
        k_sel = jax.vmap(lambda kk, ii: kk[ii])(k, idx)
        v_sel = jax.vmap(lambda vv, ii: vv[ii])(v, idx)
        s = jnp.einsum('bthd,btkd->bthk', qb, k_sel).astype(f32) * (A_HEAD_DIM ** -0.5)
        s = jnp.where(valid[:, :, None, :], s, -jnp.inf)
        p = jax.nn.softmax(s, axis=-1).astype(v.dtype)
        return jnp.einsum('bthk,btkd->bthd', p, v_sel)

    qpos = jnp.arange(seq).reshape(nb, Q_BLOCK)
    out = lax.map(one_block, (blocks(q), blocks(iq), blocks(iw), qpos))
    return jnp.swapaxes(out, 0, 1).reshape(bsz, seq, A_W)


def causal_conv_silu(x, w):
    ch = x.shape[-1]
    y = lax.conv_general_dilated(x, w[:, None, :].astype(x.dtype), window_strides=(1,),
                                 padding=[(CONV_WIDTH - 1, 0)],
                                 dimension_numbers=('NWC', 'WIO', 'NWC'),
                                 feature_group_count=ch)
    return jax.nn.silu(y)


def gated_delta_rule(q, k, v, g, beta):
    bsz, seq, nh, dk = q.shape
    dv = v.shape[-1]
    c = GDN_CHUNK
    f32 = jnp.float32
    q = to_chunks(q, c) * (dk ** -0.5)
    k = to_chunks(k, c)
    v = to_chunks(v, c)
    beta = to_chunks(beta, c)
    g = jnp.cumsum(to_chunks(g, c), axis=-1)
    row = jnp.arange(c)
    incl = row[:, None] >= row[None, :]
    strict = row[:, None] > row[None, :]
    decay = jnp.exp(jnp.where(incl, g[..., :, None] - g[..., None, :], -jnp.inf))
    k_beta = k * beta[..., None]
    m = jnp.where(strict, jnp.einsum('bhncd,bhnmd->bhncm', k_beta, k) * decay, 0.0)
    eye = jnp.eye(c, dtype=f32)
    t_inv = lax.linalg.triangular_solve(eye + m, jnp.broadcast_to(eye, m.shape),
                                        left_side=True, lower=True, unit_diagonal=True)
    u = jnp.einsum('bhncm,bhnme->bhnce', t_inv, v * beta[..., None])
    w = jnp.einsum('bhncm,bhnmd->bhncd', t_inv, k_beta * jnp.exp(g)[..., None])
    a_intra = jnp.einsum('bhncd,bhnmd->bhncm', q, k) * decay
    xs = tuple(jnp.moveaxis(a, 2, 0) for a in (q, k, u, w, g, a_intra))

    def step(state, inp):
        q_i, k_i, u_i, w_i, g_i, a_i = inp
        v_new = u_i - jnp.einsum('bhcd,bhde->bhce', w_i, state)
        o = (jnp.einsum('bhcd,bhde->bhce', q_i * jnp.exp(g_i)[..., None], state)
             + jnp.einsum('bhcm,bhme->bhce', a_i, v_new))
        g_last = g_i[..., -1:]
        state = (state * jnp.exp(g_last)[..., None]
                 + jnp.einsum('bhcd,bhce->bhde', k_i * jnp.exp(g_last - g_i)[..., None], v_new))
        return state, o

    s0 = jnp.zeros((bsz, nh, dk, dv), f32)
    _, o = lax.scan(step, s0, xs)
    o = jnp.moveaxis(o, 0, 2)
    return jnp.moveaxis(o, 1, 3).reshape(bsz, seq, nh, dv)


def retention(q, k, v):
    bsz, seq, nh, dk = q.shape
    dv = v.shape[-1]
    c = RET_CHUNK
    f32 = jnp.float32
    q = to_chunks(q, c)
    k = to_chunks(k, c) * (dk ** -0.5)
    v = to_chunks(v, c)
    log_gamma = jnp.log(1.0 - 2.0 ** (-5.0 - jnp.arange(nh, dtype=f32)))
    row = jnp.arange(c, dtype=f32)
    rel = row[:, None] - row[None, :]
    d_intra = jnp.where(rel >= 0, jnp.exp(jnp.maximum(rel, 0.0)[None] * log_gamma[:, None, None]), 0.0)
    scores = jnp.einsum('bhncd,bhnmd->bhncm', q, k) * d_intra[None, :, None]
    inner = jnp.einsum('bhncm,bhnme->bhnce', scores, v)
    xi = jnp.exp((row + 1.0)[None] * log_gamma[:, None])
    zeta = jnp.exp((c - 1.0 - row)[None] * log_gamma[:, None])
    g_chunk = jnp.exp(c * log_gamma)
    xs = tuple(jnp.moveaxis(a, 2, 0) for a in (q, k, v))

    def step(state, inp):
        q_i, k_i, v_i = inp
        cross = jnp.einsum('bhcd,bhde->bhce', q_i, state) * xi[None, :, :, None]
        state = (state * g_chunk[None, :, None, None]
                 + jnp.einsum('bhcd,bhce->bhde', k_i, v_i * zeta[None, :, :, None]))
        return state, cross

    s0 = jnp.zeros((bsz, nh, dk, dv), f32)
    _, cross = lax.scan(step, s0, xs)
    o = inner + jnp.moveaxis(cross, 0, 2)
    return jnp.moveaxis(o, 1, 3).reshape(bsz, seq, nh, dv)


def hybrid_layer(x, ln_mix_pre, w_in, gdn_conv, gdn_a_log, gdn_dt_bias, gdn_norm,
                 w_branch, w_out, ln_mix_post, ln_mlp_pre, w_up, w_down, ln_mlp_post):
    bsz, seq, _ = x.shape
    dt = x.dtype
    h = rms_norm(x, ln_mix_pre)
    (a_q, a_k, a_v, i_q, i_k, i_w,
     b_q, b_k, b_v, b_a, b_b, b_z,
     c_q, c_k, c_v, c_g, gate_logits) = split_projection(h @ w_in)

    ang_a = rope_angles(seq, A_HEAD_DIM)
    ang_i = rope_angles(seq, IDX_DIM)
    qa = rotate(a_q.reshape(bsz, seq, A_HEADS, A_HEAD_DIM), ang_a)
    ka = rotate(a_k[:, :, None, :], ang_a)[:, :, 0]
    iq = rotate(i_q.reshape(bsz, seq, IDX_HEADS, IDX_DIM), ang_i)
    ik = rotate(i_k[:, :, None, :], ang_i)[:, :, 0]
    y_a = dsa_attention(qa, ka, a_v, iq, ik, i_w)

    qkv = causal_conv_silu(jnp.concatenate([b_q, b_k, b_v], axis=-1), gdn_conv)
    qb, kb, vb = jnp.split(qkv, [B_W, 2 * B_W], axis=-1)
    qb = l2_norm(qb.reshape(bsz, seq, B_HEADS, B_HEAD_DIM))
    kb = l2_norm(kb.reshape(bsz, seq, B_HEADS, B_HEAD_DIM))
    vb = vb.reshape(bsz, seq, B_HEADS, B_HEAD_DIM)
    g = -jnp.exp(gdn_a_log.astype(jnp.float32)) * jax.nn.softplus(b_a.astype(jnp.float32) + gdn_dt_bias.astype(jnp.float32))
    beta = jax.nn.sigmoid(b_b.astype(jnp.float32))
    o_b = gated_delta_rule(qb, kb, vb, g, beta).astype(dt)
    y_b = (rms_norm(o_b, gdn_norm) * jax.nn.silu(b_z.reshape(bsz, seq, B_HEADS, B_HEAD_DIM))).reshape(bsz, seq, B_W)

    ang_c = retnet_angles(seq, C_KEY_DIM)
    qc = rotate(c_q.reshape(bsz, seq, C_HEADS, C_KEY_DIM), ang_c)
    kc = rotate(c_k.reshape(bsz, seq, C_HEADS, C_KEY_DIM), ang_c)
    vc = c_v.reshape(bsz, seq, C_HEADS, C_VAL_DIM)
    o_c = retention(qc, kc, vc).astype(dt)
    y_c = (rms_norm(o_c) * jax.nn.silu(c_g.reshape(bsz, seq, C_HEADS, C_VAL_DIM))).reshape(bsz, seq, C_W)

    gates = jax.nn.sigmoid(gate_logits.astype(jnp.float32)).astype(dt).reshape(bsz, seq, N_BRANCH, D_MODEL)
    w_ba, w_bb, w_bc = jnp.split(w_branch, [A_W, A_W + B_W], axis=0)
    merged = (gates[:, :, 0] * (y_a @ w_ba)
              + gates[:, :, 1] * (y_b @ w_bb)
              + gates[:, :, 2] * (y_c @ w_bc))
    x = x + rms_norm(merged @ w_out, ln_mix_post)

    h2 = rms_norm(x, ln_mlp_pre)
    ff = jnp.square(jax.nn.relu(h2 @ w_up)) @ w_down
    return x + rms_norm(ff, ln_mlp_post)


def setup_inputs(seed: int = 0) -> dict:
    key = jax.random.key(seed)
    ks = jax.random.split(key, 16)
    f32 = jnp.float32

    def gain(k, width):
        return 1.0 + 0.05 * jax.random.normal(k, (DEPTH, width), f32)

    x = jax.random.normal(ks[0], (BATCH, SEQ, D_MODEL), f32)
    ln_mix_pre = gain(ks[1], D_MODEL)
    w_in = jax.random.normal(ks[2], (DEPTH, D_MODEL, IN_WIDTH), f32) * D_MODEL ** -0.5
    gdn_conv = jax.random.normal(ks[3], (DEPTH, CONV_WIDTH, 3 * B_W), f32) * CONV_WIDTH ** -0.5
    gdn_a_log = jnp.log(jax.random.uniform(ks[4], (DEPTH, B_HEADS), f32, 1.0, 16.0))
    dt_init = jnp.exp(jax.random.uniform(ks[5], (DEPTH, B_HEADS), f32, math.log(1e-3), math.log(1e-1)))
    gdn_dt_bias = dt_init + jnp.log(-jnp.expm1(-dt_init))
    gdn_norm = gain(ks[6], B_HEAD_DIM)
    w_branch = jax.random.normal(ks[7], (DEPTH, A_W + B_W + C_W, D_MODEL), f32) * A_W ** -0.5
    w_out = jax.random.normal(ks[8], (DEPTH, D_MODEL, D_MODEL), f32) * D_MODEL ** -0.5
    ln_mix_post = gain(ks[9], D_MODEL)
    ln_mlp_pre = gain(ks[10], D_MODEL)
    w_up = jax.random.normal(ks[11], (DEPTH, D_MODEL, D_FF), f32) * D_MODEL ** -0.5
    w_down = jax.random.normal(ks[12], (DEPTH, D_FF, D_MODEL), f32) * D_FF ** -0.5
    ln_mlp_post = gain(ks[13], D_MODEL)
    return {"x": x, "ln_mix_pre": ln_mix_pre, "w_in": w_in, "gdn_conv": gdn_conv,
            "gdn_a_log": gdn_a_log, "gdn_dt_bias": gdn_dt_bias, "gdn_norm": gdn_norm,
            "w_branch": w_branch, "w_out": w_out, "ln_mix_post": ln_mix_post,
            "ln_mlp_pre": ln_mlp_pre, "w_up": w_up, "w_down": w_down, "ln_mlp_post": ln_mlp_post}


def reference(x, ln_mix_pre, w_in, gdn_conv, gdn_a_log, gdn_dt_bias, gdn_norm,
              w_branch, w_out, ln_mix_post, ln_mlp_pre, w_up, w_down, ln_mlp_post):
    for l in range(DEPTH):
        x = hybrid_layer(x, ln_mix_pre[l], w_in[l], gdn_conv[l], gdn_a_log[l], gdn_dt_bias[l],
                         gdn_norm[l], w_branch[l], w_out[l], ln_mix_post[l], ln_mlp_pre[l],
                         w_up[l], w_down[l], ln_mlp_post[l])
    return x
```

```python
import functools
import math

import numpy as np
import jax
import jax.numpy as jnp
from jax import lax
from jax.experimental import pallas as pl
from jax.experimental.pallas import tpu as pltpu

F32 = jnp.float32
BF16 = jnp.bfloat16
I32 = jnp.int32

D_MODEL = 1024
A_HEADS = 4
A_HEAD_DIM = 128
IDX_HEADS = 8
IDX_DIM = 64
TOPK_MAX = 256
B_HEADS = 4
B_HEAD_DIM = 128
CONV_WIDTH = 4
C_HEADS = 4
C_KEY_DIM = 64
C_VAL_DIM = 128
D_FF = 4 * D_MODEL
ROPE_THETA = 10000.0
EPS = 1e-6
N_BRANCH = 3
A_W = A_HEADS * A_HEAD_DIM
B_W = B_HEADS * B_HEAD_DIM
C_W = C_HEADS * C_VAL_DIM
C_QK = C_HEADS * C_KEY_DIM

LANES = 128
BLK = 128
GDN_BLK = 128
GDN_CHUNK = 64
GDN_BASE = 8
RET_BLK = 128
VMEM_LIMIT = 52 * 1024 * 1024

_OFF = {}
_o = 0
for _name, _w in (("a_q", A_W), ("a_k", A_HEAD_DIM), ("a_v", A_HEAD_DIM),
                  ("i_q", IDX_HEADS * IDX_DIM), ("i_k", IDX_DIM), ("i_w", IDX_HEADS),
                  ("b_q", B_W), ("b_k", B_W), ("b_v", B_W), ("b_a", B_HEADS), ("b_b", B_HEADS),
                  ("b_z", B_W), ("c_q", C_QK), ("c_k", C_QK), ("c_v", C_W), ("c_g", C_W),
                  ("gate", N_BRANCH * D_MODEL)):
    _OFF[_name] = (_o, _w)
    _o += _w

P_WIDTH = 8192
P_TILE = 1024
SM_IW = 0
SM_BA = 8
SM_BB = 12

KEY_NEG_INF = -2139095041
INT_MIN = -2147483648


def _cols(w, name):
    o, n = _OFF[name]
    return w[:, o:o + n]


def _prep_w_in(w):
    ik = _cols(w, "i_k")
    big = jnp.concatenate([
        _cols(w, "a_q"), _cols(w, "a_k"), _cols(w, "a_v"), ik, ik, ik, ik,
        _cols(w, "i_q"),
        _cols(w, "b_q"), _cols(w, "b_k"), _cols(w, "b_v"), _cols(w, "b_z"),
        _cols(w, "c_q"), _cols(w, "c_k"), _cols(w, "c_v"), _cols(w, "c_g"),
        _cols(w, "gate")], axis=1)
    assert big.shape[1] == P_WIDTH
    small = jnp.concatenate([
        _cols(w, "i_w"), _cols(w, "b_a"), _cols(w, "b_b"),
        jnp.zeros((w.shape[0], LANES - 16), w.dtype)], axis=1)
    return big.astype(BF16), small.astype(BF16)


def _nt_dot(a, b):
    return lax.dot_general(a, b, (((1,), (1,)), ((), ())), preferred_element_type=F32)


def _dot(a, b):
    return jnp.dot(a, b, preferred_element_type=F32)


def _sigmoid(x):
    return 1.0 / (1.0 + jnp.exp(-x))


def _silu(x):
    return x * _sigmoid(x)


def _softplus(x):
    return jnp.maximum(x, 0.0) + jnp.log(1.0 + jnp.exp(-jnp.abs(x)))


def _rot_half64(x, cos, sin_signed):
    return x * cos + pltpu.roll(x, 64, 1) * sin_signed


def _rot_half32(x, cos, sin_signed, first):
    partner = jnp.where(first, pltpu.roll(x, 96, 1), pltpu.roll(x, 32, 1))
    return x * cos + partner * sin_signed


def _proj_body(x_ref, gain_ref, w_ref, ws_ref, p_ref, s_ref, h_scr):
    @pl.when(pl.program_id(1) == 0)
    def _():
        x = x_ref[...]
        ms = jnp.mean(x * x, axis=-1, keepdims=True)
        h = (x * lax.rsqrt(ms + EPS) * gain_ref[...]).astype(BF16)
        h_scr[...] = h
        s_ref[...] = _dot(h, ws_ref[...])

    p_ref[...] = _dot(h_scr[...], w_ref[...]).astype(p_ref.dtype)


def _proj_call(x2d, gain, w_big, w_small):
    n = x2d.shape[0]
    tm = min(1024, n)
    grid = (n // tm, P_WIDTH // P_TILE)
    return pl.pallas_call(
        _proj_body,
        grid=grid,
        in_specs=[
            pl.BlockSpec((tm, D_MODEL), lambda i, j: (i, 0)),
            pl.BlockSpec((1, D_MODEL), lambda i, j: (0, 0)),
            pl.BlockSpec((D_MODEL, P_TILE), lambda i, j: (0, j)),
            pl.BlockSpec((D_MODEL, LANES), lambda i, j: (0, 0)),
        ],
        out_specs=[
            pl.BlockSpec((tm, P_TILE), lambda i, j: (i, j)),
            pl.BlockSpec((tm, LANES), lambda i, j: (i, 0)),
        ],
        out_shape=[
            jax.ShapeDtypeStruct((n, P_WIDTH), BF16),
            jax.ShapeDtypeStruct((n, LANES), F32),
        ],
        scratch_shapes=[pltpu.VMEM((tm, D_MODEL), BF16)],
        compiler_params=pltpu.CompilerParams(
            dimension_semantics=("arbitrary", "arbitrary"), vmem_limit_bytes=VMEM_LIMIT),
        name="proj",
    )(x2d, gain, w_big, w_small)


def _dsa_body(n_sel, aq_ref, kv_ref, iq_ref, sm_ref,
              cosa_ref, sina_ref, cosi_ref, sini_ref,
              cosaq_ref, sinaq_ref, cosiq_ref, siniq_ref,
              y_ref, ka_scr, ki_scr, vt_scr, key_scr, s_scr, acc_scr):
    j = pl.program_id(1)
    nkb = ka_scr.shape[0]
    seq = nkb * BLK
    lane = lax.broadcasted_iota(I32, (BLK, LANES), 1)
    row = lax.broadcasted_iota(I32, (BLK, LANES), 0)
    first = (lane & 32) == 0
    lo_head = lane < 64

    @pl.when(j == 0)
    def _():
        def prep(kb, c):
            r0 = pl.multiple_of(kb * BLK, BLK)
            kv = kv_ref[0, pl.ds(r0, BLK), :].astype(F32)
            ca = cosa_ref[pl.ds(r0, BLK), :]
            sa = sina_ref[pl.ds(r0, BLK), :]
            ci = cosi_ref[pl.ds(r0, BLK), :]
            si = sini_ref[pl.ds(r0, BLK), :]
            ka_scr[kb] = _rot_half64(kv[:, 0:128], ca, sa).astype(BF16)
            ki_scr[kb] = _rot_half32(kv[:, 256:384], ci, si, first).astype(BF16)
            vt_scr[kb] = kv[:, 128:256].T.astype(BF16)
            return c
        lax.fori_loop(0, nkb, prep, 0)

    aq = aq_ref[0].astype(F32)
    caq = cosaq_ref[...]
    saq = sinaq_ref[...]
    qa = jnp.concatenate(
        [(_rot_half64(aq[:, h * 128:(h + 1) * 128], caq, saq) * (A_HEAD_DIM ** -0.5)).astype(BF16)
         for h in range(A_HEADS)], axis=0)

    iq = iq_ref[0].astype(F32)
    ciq = cosiq_ref[...]
    siq = siniq_ref[...]
    rows = []
    for p in range(IDX_HEADS // 2):
        rp = _rot_half32(iq[:, p * 128:(p + 1) * 128], ciq, siq, first)
        rows.append(jnp.where(lo_head, rp, 0.0).astype(BF16))
        rows.append(jnp.where(lo_head, 0.0, rp).astype(BF16))
    qm = jnp.concatenate(rows, axis=0)

    w_t = sm_ref[0].T * ((IDX_DIM ** -0.5) * (IDX_HEADS ** -0.5))

    def score_blk(kb, c):
        lg = _nt_dot(ki_scr[kb], qm)
        sc = jnp.zeros((BLK, LANES), F32)
        for h in range(IDX_HEADS):
            sc = sc + jnp.maximum(lg[:, h * BLK:(h + 1) * BLK], 0.0) * w_t[SM_IW + h:SM_IW + h + 1, :]
        sc = jnp.where(sc == 0.0, 0.0, sc)
        bits = pltpu.bitcast(sc, I32)
        key = bits ^ ((bits >> 31) & 0x7FFFFFFF)
        causal = (kb * BLK + row) <= (j * BLK + lane)
        key_scr[kb] = jnp.where(causal, key, KEY_NEG_INF)
        return c
    lax.fori_loop(0, j + 1, score_blk, 0)

    def count(pred):
        def body(kb, acc):
            idx = kb * BLK + row
            m = pred(key_scr[kb], idx).astype(I32)
            return acc + m.reshape(BLK // 8, 8, LANES).sum(axis=0)
        acc = lax.fori_loop(0, j + 1, body, jnp.zeros((8, LANES), I32))
        return acc.sum(axis=0, keepdims=True)

    def search():
        def bit_step(i, tu):
            cand = tu | lax.shift_left(jnp.int32(1), 31 - i)
            cs = cand ^ INT_MIN
            cnt = count(lambda k, idx: k >= cs)
            return jnp.where(cnt >= n_sel, cand, tu)
        tu = lax.fori_loop(0, 32, bit_step, jnp.zeros((1, LANES), I32))
        t = tu ^ INT_MIN
        need = n_sel - count(lambda k, idx: k > t)
        n_eq = count(lambda k, idx: k == t)

        def tie_break():
            nbits = max(1, (seq - 1).bit_length())

            def jbit(i, j0):
                cand = j0 | lax.shift_left(jnp.int32(1), nbits - 1 - i)
                g = count(lambda k, idx: (k == t) & (idx < cand))
                return jnp.where(g < need, cand, j0)
            return lax.fori_loop(0, nbits, jbit, jnp.zeros((1, LANES), I32))

        j0 = lax.cond(jnp.all(n_eq == need), lambda: jnp.full((1, LANES), seq, I32), tie_break)
        return t, j0

    def take_all():
        return jnp.full((1, LANES), KEY_NEG_INF, I32), jnp.full((1, LANES), -1, I32)

    t, j0 = lax.cond((j + 1) * BLK > n_sel, search, take_all)

    def logit_blk(kb, macc):
        s = _nt_dot(ka_scr[kb], qa)
        key = key_scr[kb]
        idx = kb * BLK + row
        sel = (key > t) | ((key == t) & (idx <= j0))
        sm = jnp.concatenate(
            [jnp.where(sel, s[:, h * BLK:(h + 1) * BLK], -jnp.inf) for h in range(A_HEADS)], axis=1)
        s_scr[kb] = sm
        return jnp.maximum(macc, sm.reshape(BLK // 8, 8, A_HEADS * BLK).max(axis=0))
    macc = lax.fori_loop(0, j + 1, logit_blk, jnp.full((8, A_HEADS * BLK), -jnp.inf, F32))
    m = macc.max(axis=0, keepdims=True)

    acc_scr[...] = jnp.zeros_like(acc_scr)

    def pv_blk(kb, lacc):
        p = jnp.exp(s_scr[kb] - m)
        acc_scr[...] += _dot(vt_scr[kb], p.astype(BF16))
        return lacc + p.reshape(BLK // 8, 8, A_HEADS * BLK).sum(axis=0)
    lacc = lax.fori_loop(0, j + 1, pv_blk, jnp.zeros((8, A_HEADS * BLK), F32))
    inv_l = 1.0 / lacc.sum(axis=0, keepdims=True)
    o_t = acc_scr[...] * inv_l
    for h in range(A_HEADS):
        y_ref[0, :, h * 128:(h + 1) * 128] = o_t[:, h * BLK:(h + 1) * BLK].T.astype(y_ref.dtype)


def _dsa_call(p3, sm3, tabs, n_sel):
    bsz, seq, _ = p3.shape
    nkb = seq // BLK
    cosa, sina, cosi, sini = tabs
    full = lambda b, j: (0, 0)
    qblk = lambda b, j: (j, 0)
    return pl.pallas_call(
        functools.partial(_dsa_body, n_sel),
        grid=(bsz, nkb),
        in_specs=[
            pl.BlockSpec((1, BLK, 512), lambda b, j: (b, j, 0)),
            pl.BlockSpec((1, seq, 512), lambda b, j: (b, 0, 1)),
            pl.BlockSpec((1, BLK, 512), lambda b, j: (b, j, 2)),
            pl.BlockSpec((1, BLK, LANES), lambda b, j: (b, j, 0)),
            pl.BlockSpec((seq, LANES), full), pl.BlockSpec((seq, LANES), full),
            pl.BlockSpec((seq, LANES), full), pl.BlockSpec((seq, LANES), full),
            pl.BlockSpec((BLK, LANES), qblk), pl.BlockSpec((BLK, LANES), qblk),
            pl.BlockSpec((BLK, LANES), qblk), pl.BlockSpec((BLK, LANES), qblk),
        ],
        out_specs=pl.BlockSpec((1, BLK, A_W), lambda b, j: (b, j, 0)),
        out_shape=jax.ShapeDtypeStruct((bsz, seq, A_W), BF16),
        scratch_shapes=[
            pltpu.VMEM((nkb, BLK, 128), BF16),
            pltpu.VMEM((nkb, BLK, 128), BF16),
            pltpu.VMEM((nkb, 128, BLK), BF16),
            pltpu.VMEM((nkb, BLK, LANES), I32),
            pltpu.VMEM((nkb, BLK, A_HEADS * BLK), F32),
            pltpu.VMEM((128, A_HEADS * BLK), F32),
        ],
        compiler_params=pltpu.CompilerParams(
            dimension_semantics=("arbitrary", "arbitrary"), vmem_limit_bytes=VMEM_LIMIT),
        name="dsa",
    )(p3, p3, p3, sm3, cosa, sina, cosi, sini, cosa, sina, cosi, sini)


def _gdn_body(q_ref, k_ref, v_ref, z_ref, sm_ref, cw_ref, alog_ref, dtb_ref, gn_ref,
              y_ref, s_scr):
    c = GDN_BLK
    hc = GDN_CHUNK
    nc = q_ref.shape[1] // c
    row = lax.broadcasted_iota(I32, (c, LANES), 0)
    col = lax.broadcasted_iota(I32, (c, LANES), 1)

    def same_block(size):
        sh = int(math.log2(size))
        return (row >> sh) == (col >> sh)

    same_chunk = same_block(hc)
    incl = same_chunk & (row >= col)
    strict = same_chunk & (row > col)
    eye = (row == col).astype(F32)
    sizes = [GDN_BASE * 2 ** i for i in range(int(math.log2(hc // GDN_BASE)) + 1)]
    base_mask = same_block(GDN_BASE) & (row > col)
    level_masks = [same_block(big) & jnp.logical_not(same_block(small))
                   for small, big in zip(sizes[:-1], sizes[1:])]
    top_rows = row < hc
    s_scr[...] = jnp.zeros_like(s_scr)
    row_w = lax.broadcasted_iota(I32, (c, B_W), 0)

    def split(a):
        hi = a.astype(BF16)
        return hi, (a - hi.astype(F32)).astype(BF16)

    def mm3(a, b):
        ah, al = a
        bh, bl = b
        return _dot(ah, bh) + (_dot(ah, bl) + _dot(al, bh))

    def unit_lower_inverse(m):
        n1 = jnp.where(base_mask, -m, 0.0)
        d = eye + n1
        pw = split(n1)
        for _ in range(int(math.log2(GDN_BASE)) - 1):
            pw = split(mm3(pw, pw))
            d = d + mm3(split(d), pw)
        for lm in level_masks:
            ds = split(d)
            d = d - mm3(split(mm3(ds, split(jnp.where(lm, m, 0.0)))), ds)
        return d

    def conv_silu(ref, n, wofs):
        r0 = pl.multiple_of(n * c, c)
        cur = ref[0, pl.ds(r0, c), :].astype(F32)
        pr0 = pl.multiple_of(jnp.maximum(n - 1, 0) * c, c)
        prev = ref[0, pl.ds(pr0, c), :].astype(F32) * (n > 0).astype(F32)
        y = cur * cw_ref[CONV_WIDTH - 1:CONV_WIDTH, wofs:wofs + B_W]
        for s in range(1, CONV_WIDTH):
            sh = jnp.where(row_w < s, pltpu.roll(prev, s, 0), pltpu.roll(cur, s, 0))
            y = y + sh * cw_ref[CONV_WIDTH - 1 - s:CONV_WIDTH - s, wofs:wofs + B_W]
        return _silu(y)

    def chunk(n, carry):
        r0 = pl.multiple_of(n * c, c)
        qc = conv_silu(q_ref, n, 0)
        kc = conv_silu(k_ref, n, B_W)
        vc = conv_silu(v_ref, n, 2 * B_W)
        sm = sm_ref[0, pl.ds(r0, c), :]
        g = -jnp.exp(alog_ref[...]) * _softplus(sm + dtb_ref[...])
        beta = _sigmoid(sm)
        s = 1
        while s < hc:
            g = g + jnp.where((row & (hc - 1)) >= s, pltpu.roll(g, s, 0), 0.0)
            s *= 2
        g_t = g.T
        for h in range(B_HEADS):
            sl = slice(h * 128, (h + 1) * 128)
            gcol = jnp.broadcast_to(g[:, SM_BA + h:SM_BA + h + 1], (c, LANES))
            grow = jnp.broadcast_to(g_t[SM_BA + h:SM_BA + h + 1, :], (c, LANES))
            bcol = jnp.broadcast_to(beta[:, SM_BB + h:SM_BB + h + 1], (c, LANES))
            qh = qc[:, sl]
            kh = kc[:, sl]
            vh = vc[:, sl]
            qh = qh * lax.rsqrt(jnp.sum(qh * qh, axis=-1, keepdims=True) + EPS) * (B_HEAD_DIM ** -0.5)
            kh = kh * lax.rsqrt(jnp.sum(kh * kh, axis=-1, keepdims=True) + EPS)
            decay = jnp.exp(jnp.where(incl, gcol - grow, -jnp.inf))
            kbeta = kh * bcol
            khb = kh.astype(BF16)
            prod = _nt_dot(jnp.concatenate([kbeta.astype(BF16), qh.astype(BF16)], axis=0), khb)
            m = jnp.where(strict, prod[:c] * decay, 0.0)
            a_intra = (prod[c:] * decay).astype(BF16)
            inv = unit_lower_inverse(m)
            egc = jnp.exp(gcol)
            rhs = jnp.concatenate([(vh * bcol).astype(BF16), (kbeta * egc).astype(BF16)], axis=1)
            uw = _dot(inv.astype(BF16), rhs)
            u = uw[:, :128]
            wb = uw[:, 128:].astype(BF16)
            qg = (qh * egc).astype(BF16)
            g_last = jnp.where(top_rows, gcol[hc - 1:hc, :], gcol[c - 1:c, :])
            kd_t = (kh * jnp.exp(g_last - gcol)).T
            st = s_scr[h]
            o_parts = []
            for ci in range(c // hc):
                rs = slice(ci * hc, (ci + 1) * hc)
                stb = st.astype(BF16)
                v_new = u[rs] - _dot(wb[rs], stb)
                vn_pad = jnp.concatenate([v_new] * (c // hc), axis=0).astype(BF16)
                o_parts.append(_dot(qg[rs], stb) + _dot(a_intra[rs], vn_pad))
                in_chunk = (col >> int(math.log2(hc))) == ci
                kd_ci = jnp.where(in_chunk, kd_t, 0.0).astype(BF16)
                st = st * jnp.exp(gcol[(ci + 1) * hc - 1:(ci + 1) * hc, :]) + _dot(kd_ci, vn_pad)
            s_scr[h] = st
            o = jnp.concatenate(o_parts, axis=0)
            zo = z_ref[0, pl.ds(r0, c), sl].astype(F32)
            on = o * lax.rsqrt(jnp.mean(o * o, axis=-1, keepdims=True) + EPS) * gn_ref[...]
            y_ref[0, pl.ds(r0, c), sl] = (on * _silu(zo)).astype(y_ref.dtype)
        return carry

    lax.fori_loop(0, nc, chunk, 0)


def _gdn_call(p3, sm3, conv_w, alog_v, dtb_v, gnorm):
    bsz, seq, _ = p3.shape
    col = lambda g: (lambda b: (b, 0, g))
    vec = lambda b: (0, 0)
    return pl.pallas_call(
        _gdn_body,
        grid=(bsz,),
        in_specs=[
            pl.BlockSpec((1, seq, B_W), col(3)), pl.BlockSpec((1, seq, B_W), col(4)),
            pl.BlockSpec((1, seq, B_W), col(5)), pl.BlockSpec((1, seq, B_W), col(6)),
            pl.BlockSpec((1, seq, LANES), lambda b: (b, 0, 0)),
            pl.BlockSpec((CONV_WIDTH, 3 * B_W), vec),
            pl.BlockSpec((1, LANES), vec), pl.BlockSpec((1, LANES), vec), pl.BlockSpec((1, LANES), vec),
        ],
        out_specs=pl.BlockSpec((1, seq, B_W), lambda b: (b, 0, 0)),
        out_shape=jax.ShapeDtypeStruct((bsz, seq, B_W), BF16),
        scratch_shapes=[pltpu.VMEM((B_HEADS, 128, 128), F32)],
        compiler_params=pltpu.CompilerParams(
            dimension_semantics=("arbitrary",), vmem_limit_bytes=VMEM_LIMIT),
        name="gdn",
    )(p3, p3, p3, p3, sm3, conv_w, alog_v, dtb_v, gnorm)


def _ret_body(qk_ref, v_ref, g_ref, cos_ref, sin_ref, dmat_ref, xi_ref, zeta_ref, gch_ref,
              y_ref, s_scr):
    c = RET_BLK
    nc = v_ref.shape[1] // c
    lane = lax.broadcasted_iota(I32, (c, LANES), 1)
    first = (lane & 32) == 0
    lo_head = lane < 64
    s_scr[...] = jnp.zeros_like(s_scr)

    def chunk(n, carry):
        r0 = pl.multiple_of(n * c, c)
        qk = qk_ref[0, pl.ds(r0, c), :].astype(F32)
        cs = cos_ref[pl.ds(r0, c), :]
        sn = sin_ref[pl.ds(r0, c), :]
        for p in range(C_HEADS // 2):
            qp = _rot_half32(qk[:, p * 128:(p + 1) * 128], cs, sn, first)
            kp = _rot_half32(qk[:, C_QK + p * 128:C_QK + (p + 1) * 128], cs, sn, first) * (C_KEY_DIM ** -0.5)
            qm = [jnp.where(lo_head, qp, 0.0).astype(BF16), jnp.where(lo_head, 0.0, qp).astype(BF16)]
            km = [jnp.where(lo_head, kp, 0.0), jnp.where(lo_head, 0.0, kp)]
            sc = _nt_dot(jnp.concatenate(qm, axis=0), kp.astype(BF16))
            for e in range(2):
                h = 2 * p + e
                sl = slice(h * 128, (h + 1) * 128)
                vh = v_ref[0, pl.ds(r0, c), sl].astype(F32)
                scores = sc[e * c:(e + 1) * c] * dmat_ref[h]
                st = s_scr[h]
                o = _dot(scores.astype(BF16), vh.astype(BF16)) + _dot(qm[e], st.astype(BF16)) * xi_ref[h]
                s_scr[h] = st * gch_ref[h, 0:1, :] + _dot(km[e].T.astype(BF16), (vh * zeta_ref[h]).astype(BF16))
                on = o * lax.rsqrt(jnp.mean(o * o, axis=-1, keepdims=True) + EPS)
                go = g_ref[0, pl.ds(r0, c), sl].astype(F32)
                y_ref[0, pl.ds(r0, c), sl] = (on * _silu(go)).astype(y_ref.dtype)
        return carry

    lax.fori_loop(0, nc, chunk, 0)


def _ret_call(p3, cos_t, sin_t, dmat, xi, zeta, gch):
    bsz, seq, _ = p3.shape
    c = RET_BLK
    z2 = lambda b: (0, 0)
    z3 = lambda b: (0, 0, 0)
    return pl.pallas_call(
        _ret_body,
        grid=(bsz,),
        in_specs=[
            pl.BlockSpec((1, seq, 512), lambda b: (b, 0, 7)),
            pl.BlockSpec((1, seq, 512), lambda b: (b, 0, 8)),
            pl.BlockSpec((1, seq, 512), lambda b: (b, 0, 9)),
            pl.BlockSpec((seq, LANES), z2), pl.BlockSpec((seq, LANES), z2),
            pl.BlockSpec((C_HEADS, c, c), z3),
            pl.BlockSpec((C_HEADS, c, LANES), z3), pl.BlockSpec((C_HEADS, c, LANES), z3),
            pl.BlockSpec((C_HEADS, 8, LANES), z3),
        ],
        out_specs=pl.BlockSpec((1, seq, C_W), lambda b: (b, 0, 0)),
        out_shape=jax.ShapeDtypeStruct((bsz, seq, C_W), BF16),
        scratch_shapes=[pltpu.VMEM((C_HEADS, 128, 128), F32)],
        compiler_params=pltpu.CompilerParams(
            dimension_semantics=("arbitrary",), vmem_limit_bytes=VMEM_LIMIT),
        name="ret",
    )(p3, p3, p3, cos_t, sin_t, dmat, xi, zeta, gch)


def _merge_body(ya_ref, yb_ref, yc_ref, g0_ref, g1_ref, g2_ref, x_ref, wb_ref, wo_ref, ln_ref, o_ref):
    def gate(ref):
        return _sigmoid(ref[...].astype(F32))
    merged = (gate(g0_ref) * _dot(ya_ref[...], wb_ref[0:A_W, :])
              + gate(g1_ref) * _dot(yb_ref[...], wb_ref[A_W:A_W + B_W, :])
              + gate(g2_ref) * _dot(yc_ref[...], wb_ref[A_W + B_W:, :]))
    o = _dot(merged.astype(BF16), wo_ref[...])
    on = o * lax.rsqrt(jnp.mean(o * o, axis=-1, keepdims=True) + EPS) * ln_ref[...]
    o_ref[...] = x_ref[...] + on


def _merge_call(ya, yb, yc, p2d, x2d, wb, wo, ln):
    n = x2d.shape[0]
    tm = min(512, n)
    rowb = lambda i: (i, 0)
    cst = lambda i: (0, 0)
    gcol = lambda g: (lambda i: (i, g))
    return pl.pallas_call(
        _merge_body,
        grid=(n // tm,),
        in_specs=[
            pl.BlockSpec((tm, A_W), rowb), pl.BlockSpec((tm, B_W), rowb), pl.BlockSpec((tm, C_W), rowb),
            pl.BlockSpec((tm, D_MODEL), gcol(5)), pl.BlockSpec((tm, D_MODEL), gcol(6)),
            pl.BlockSpec((tm, D_MODEL), gcol(7)),
            pl.BlockSpec((tm, D_MODEL), rowb),
            pl.BlockSpec((A_W + B_W + C_W, D_MODEL), cst),
            pl.BlockSpec((D_MODEL, D_MODEL), cst),
            pl.BlockSpec((1, D_MODEL), cst),
        ],
        out_specs=pl.BlockSpec((tm, D_MODEL), rowb),
        out_shape=jax.ShapeDtypeStruct((n, D_MODEL), F32),
        compiler_params=pltpu.CompilerParams(
            dimension_semantics=("arbitrary",), vmem_limit_bytes=VMEM_LIMIT),
        name="merge",
    )(ya, yb, yc, p2d, p2d, p2d, x2d, wb, wo, ln)


def _mlp_body(x_ref, lnpre_ref, wu_ref, wd_ref, lnpost_ref, o_ref):
    x = x_ref[...]
    h = (x * lax.rsqrt(jnp.mean(x * x, axis=-1, keepdims=True) + EPS) * lnpre_ref[...]).astype(BF16)
    ff = jnp.zeros(x.shape, F32)
    for cidx in range(D_FF // D_MODEL):
        sl = slice(cidx * D_MODEL, (cidx + 1) * D_MODEL)
        u = jnp.maximum(_dot(h, wu_ref[:, sl]), 0.0)
        ff = ff + _dot((u * u).astype(BF16), wd_ref[sl, :])
    o_ref[...] = x + ff * lax.rsqrt(jnp.mean(ff * ff, axis=-1, keepdims=True) + EPS) * lnpost_ref[...]


def _mlp_call(x2d, lnpre, wu, wd, lnpost):
    n = x2d.shape[0]
    tm = min(512, n)
    rowb = lambda i: (i, 0)
    cst = lambda i: (0, 0)
    return pl.pallas_call(
        _mlp_body,
        grid=(n // tm,),
        in_specs=[
            pl.BlockSpec((tm, D_MODEL), rowb),
            pl.BlockSpec((1, D_MODEL), cst),
            pl.BlockSpec((D_MODEL, D_FF), cst),
            pl.BlockSpec((D_FF, D_MODEL), cst),
            pl.BlockSpec((1, D_MODEL), cst),
        ],
        out_specs=pl.BlockSpec((tm, D_MODEL), rowb),
        out_shape=jax.ShapeDtypeStruct((n, D_MODEL), F32),
        compiler_params=pltpu.CompilerParams(
            dimension_semantics=("arbitrary",), vmem_limit_bytes=VMEM_LIMIT),
        name="mlp",
    )(x2d, lnpre, wu, wd, lnpost)


def _rope_tables(seq):
    pos = jnp.arange(seq, dtype=F32)
    inv_a = ROPE_THETA ** (-jnp.arange(0, A_HEAD_DIM, 2, dtype=F32) / A_HEAD_DIM)
    ang_a = pos[:, None] * inv_a[None, :]
    cosa = jnp.concatenate([jnp.cos(ang_a)] * 2, axis=1)
    sina = jnp.concatenate([-jnp.sin(ang_a), jnp.sin(ang_a)], axis=1)
    inv_i = ROPE_THETA ** (-jnp.arange(0, IDX_DIM, 2, dtype=F32) / IDX_DIM)
    ang_i = pos[:, None] * inv_i[None, :]
    cosi = jnp.concatenate([jnp.cos(ang_i)] * 4, axis=1)
    sini = jnp.concatenate([-jnp.sin(ang_i), jnp.sin(ang_i)] * 2, axis=1)
    inv_c = 1.0 / (ROPE_THETA ** jnp.linspace(0.0, 1.0, C_KEY_DIM // 2, dtype=F32))
    ang_c = pos[:, None] * inv_c[None, :]
    cosc = jnp.concatenate([jnp.cos(ang_c)] * 4, axis=1)
    sinc = jnp.concatenate([-jnp.sin(ang_c), jnp.sin(ang_c)] * 2, axis=1)
    return (cosa, sina, cosi, sini), (cosc, sinc)


def _ret_tables():
    c = RET_BLK
    log_gamma = jnp.log(1.0 - 2.0 ** (-5.0 - jnp.arange(C_HEADS, dtype=F32)))
    r = jnp.arange(c, dtype=F32)
    rel = r[:, None] - r[None, :]
    dmat = jnp.where(rel >= 0, jnp.exp(jnp.maximum(rel, 0.0)[None] * log_gamma[:, None, None]), 0.0)
    xi = jnp.exp((r + 1.0)[None] * log_gamma[:, None])
    zeta = jnp.exp((c - 1.0 - r)[None] * log_gamma[:, None])
    gch = jnp.exp(c * log_gamma)
    xi_b = jnp.broadcast_to(xi[:, :, None], (C_HEADS, c, LANES))
    zeta_b = jnp.broadcast_to(zeta[:, :, None], (C_HEADS, c, LANES))
    gch_b = jnp.broadcast_to(gch[:, None, None], (C_HEADS, 8, LANES))
    return dmat, xi_b, zeta_b, gch_b


def _lane_vec(vals, ofs):
    v = jnp.zeros((1, LANES), F32)
    return v.at[0, ofs:ofs + vals.shape[0]].set(vals.astype(F32))


def kernel(x, ln_mix_pre, w_in, gdn_conv, gdn_a_log, gdn_dt_bias, gdn_norm, w_branch, w_out,
           ln_mix_post, ln_mlp_pre, w_up, w_down, ln_mlp_post):
    bsz, seq, _ = x.shape
    depth = w_in.shape[0]
    n = bsz * seq
    n_sel = min(TOPK_MAX, seq // 4)
    assert seq % BLK == 0 and n_sel % BLK == 0
    dsa_tabs, (cosc, sinc) = _rope_tables(seq)
    dmat, xi_b, zeta_b, gch_b = _ret_tables()

    x2d = x.reshape(n, D_MODEL)
    for l in range(depth):
        w_big, w_small = _prep_w_in(w_in[l])
        p2d, sm2d = _proj_call(x2d, ln_mix_pre[l][None, :], w_big, w_small)
        p3 = p2d.reshape(bsz, seq, P_WIDTH)
        sm3 = sm2d.reshape(bsz, seq, LANES)
        y_a = _dsa_call(p3, sm3, dsa_tabs, n_sel)
        y_b = _gdn_call(p3, sm3, gdn_conv[l], _lane_vec(gdn_a_log[l], SM_BA),
                        _lane_vec(gdn_dt_bias[l], SM_BA), gdn_norm[l][None, :])
        y_c = _ret_call(p3, cosc, sinc, dmat, xi_b, zeta_b, gch_b)
        x2d = _merge_call(y_a.reshape(n, A_W), y_b.reshape(n, B_W), y_c.reshape(n, C_W), p2d, x2d,
                          w_branch[l].astype(BF16), w_out[l].astype(BF16), ln_mix_post[l][None, :])
        x2d = _mlp_call(x2d, ln_mlp_pre[l][None, :], w_up[l].astype(BF16), w_down[l].astype(BF16),
                        ln_mlp_post[l][None, :])
    return x2d.reshape(bsz, seq, D_MODEL)
```

```python
import functools
import math

import numpy as np
import jax
import jax.numpy as jnp
from jax import lax
from jax.experimental import pallas as pl
from jax.experimental.pallas import tpu as pltpu

F32 = jnp.float32
BF16 = jnp.bfloat16
I32 = jnp.int32

D_MODEL = 1024
A_HEADS = 4
A_HEAD_DIM = 128
IDX_HEADS = 8
IDX_DIM = 64
TOPK_MAX = 256
B_HEADS = 4
B_HEAD_DIM = 128
CONV_WIDTH = 4
C_HEADS = 4
C_KEY_DIM = 64
C_VAL_DIM = 128
D_FF = 4 * D_MODEL
ROPE_THETA = 10000.0
EPS = 1e-6
N_BRANCH = 3
A_W = A_HEADS * A_HEAD_DIM
B_W = B_HEADS * B_HEAD_DIM
C_W = C_HEADS * C_VAL_DIM
C_QK = C_HEADS * C_KEY_DIM

LANES = 128
BLK = 128
KBLK = 256
GDN_BLK = 128
GDN_CHUNK = 64
GDN_BASE = 8
RET_BLK = 128
VMEM_LIMIT = 52 * 1024 * 1024

_OFF = {}
_o = 0
for _name, _w in (("a_q", A_W), ("a_k", A_HEAD_DIM), ("a_v", A_HEAD_DIM),
                  ("i_q", IDX_HEADS * IDX_DIM), ("i_k", IDX_DIM), ("i_w", IDX_HEADS),
                  ("b_q", B_W), ("b_k", B_W), ("b_v", B_W), ("b_a", B_HEADS), ("b_b", B_HEADS),
                  ("b_z", B_W), ("c_q", C_QK), ("c_k", C_QK), ("c_v", C_W), ("c_g", C_W),
                  ("gate", N_BRANCH * D_MODEL)):
    _OFF[_name] = (_o, _w)
    _o += _w

P_WIDTH = 8192
P_TILE = 1024
SM_IW = 0
SM_BA = 8
SM_BB = 12

KEY_NEG_INF = -2139095041
INT_MIN = -2147483648


def _cols(w, name):
    o, n = _OFF[name]
    return w[:, o:o + n]


def _prep_w_in(w):
    ik = _cols(w, "i_k")
    big = jnp.concatenate([
        _cols(w, "a_q"), _cols(w, "a_k"), _cols(w, "a_v"), ik, ik, ik, ik,
        _cols(w, "i_q"),
        _cols(w, "b_q"), _cols(w, "b_k"), _cols(w, "b_v"), _cols(w, "b_z"),
        _cols(w, "c_q"), _cols(w, "c_k"), _cols(w, "c_v"), _cols(w, "c_g"),
        _cols(w, "gate")], axis=1)
    assert big.shape[1] == P_WIDTH
    small = jnp.concatenate([
        _cols(w, "i_w"), _cols(w, "b_a"), _cols(w, "b_b"),
        jnp.zeros((w.shape[0], LANES - 16), w.dtype)], axis=1)
    return big.astype(BF16), small.astype(BF16)


def _nt_dot(a, b):
    return lax.dot_general(a, b, (((1,), (1,)), ((), ())), preferred_element_type=F32)


def _dot(a, b):
    return jnp.dot(a, b, preferred_element_type=F32)


def _sigmoid(x):
    return 1.0 / (1.0 + jnp.exp(-x))


def _silu(x):
    return x * _sigmoid(x)


def _softplus(x):
    return jnp.maximum(x, 0.0) + jnp.log(1.0 + jnp.exp(-jnp.abs(x)))


def _rot_half64(x, cos, sin_signed):
    return x * cos + pltpu.roll(x, 64, 1) * sin_signed


def _rot_half32(x, cos, sin_signed, first):
    partner = jnp.where(first, pltpu.roll(x, 96, 1), pltpu.roll(x, 32, 1))
    return x * cos + partner * sin_signed


def _proj_body(x_ref, gain_ref, w_ref, ws_ref, p_ref, s_ref, h_scr):
    @pl.when(pl.program_id(1) == 0)
    def _():
        x = x_ref[...]
        ms = jnp.mean(x * x, axis=-1, keepdims=True)
        h = (x * lax.rsqrt(ms + EPS) * gain_ref[...]).astype(BF16)
        h_scr[...] = h
        s_ref[...] = _dot(h, ws_ref[...])

    p_ref[...] = _dot(h_scr[...], w_ref[...]).astype(p_ref.dtype)


def _proj_call(x2d, gain, w_big, w_small):
    n = x2d.shape[0]
    tm = min(1024, n)
    grid = (n // tm, P_WIDTH // P_TILE)
    return pl.pallas_call(
        _proj_body,
        grid=grid,
        in_specs=[
            pl.BlockSpec((tm, D_MODEL), lambda i, j: (i, 0)),
            pl.BlockSpec((1, D_MODEL), lambda i, j: (0, 0)),
            pl.BlockSpec((D_MODEL, P_TILE), lambda i, j: (0, j)),
            pl.BlockSpec((D_MODEL, LANES), lambda i, j: (0, 0)),
        ],
        out_specs=[
            pl.BlockSpec((tm, P_TILE), lambda i, j: (i, j)),
            pl.BlockSpec((tm, LANES), lambda i, j: (i, 0)),
        ],
        out_shape=[
            jax.ShapeDtypeStruct((n, P_WIDTH), BF16),
            jax.ShapeDtypeStruct((n, LANES), F32),
        ],
        scratch_shapes=[pltpu.VMEM((tm, D_MODEL), BF16)],
        compiler_params=pltpu.CompilerParams(
            dimension_semantics=("arbitrary", "arbitrary"), vmem_limit_bytes=VMEM_LIMIT),
        name="proj",
    )(x2d, gain, w_big, w_small)


def _dsa_body(n_sel, aq_ref, kv_ref, iq_ref, sm_ref,
              cosa_ref, sina_ref, cosi_ref, sini_ref,
              cosaq_ref, sinaq_ref, cosiq_ref, siniq_ref,
              y_ref, ka_scr, ki_scr, vt_scr, key_scr, s_scr, acc_scr):
    j = pl.program_id(1)
    nkb = ka_scr.shape[0]
    kblk = ka_scr.shape[1]
    seq = nkb * kblk
    nk = (j * BLK + BLK + kblk - 1) // kblk
    lane = lax.broadcasted_iota(I32, (BLK, LANES), 1)
    klane = lax.broadcasted_iota(I32, (kblk, LANES), 1)
    krow = lax.broadcasted_iota(I32, (kblk, LANES), 0)
    first = (lane & 32) == 0
    kfirst = (klane & 32) == 0
    lo_head = lane < 64

    @pl.when(j == 0)
    def _():
        def prep(kb, c):
            r0 = pl.multiple_of(kb * kblk, kblk)
            kv = kv_ref[0, pl.ds(r0, kblk), :].astype(F32)
            ca = cosa_ref[pl.ds(r0, kblk), :]
            sa = sina_ref[pl.ds(r0, kblk), :]
            ci = cosi_ref[pl.ds(r0, kblk), :]
            si = sini_ref[pl.ds(r0, kblk), :]
            ka_scr[kb] = _rot_half64(kv[:, 0:128], ca, sa).astype(BF16)
            ki_scr[kb] = _rot_half32(kv[:, 256:384], ci, si, kfirst).astype(BF16)
            vt_scr[kb] = kv[:, 128:256].T.astype(BF16)
            return c
        lax.fori_loop(0, nkb, prep, 0)

    aq = aq_ref[0].astype(F32)
    caq = cosaq_ref[...]
    saq = sinaq_ref[...]
    qa = jnp.concatenate(
        [(_rot_half64(aq[:, h * 128:(h + 1) * 128], caq, saq) * (A_HEAD_DIM ** -0.5)).astype(BF16)
         for h in range(A_HEADS)], axis=0)

    iq = iq_ref[0].astype(F32)
    ciq = cosiq_ref[...]
    siq = siniq_ref[...]
    rows = []
    for p in range(IDX_HEADS // 2):
        rp = _rot_half32(iq[:, p * 128:(p + 1) * 128], ciq, siq, first)
        rows.append(jnp.where(lo_head, rp, 0.0).astype(BF16))
        rows.append(jnp.where(lo_head, 0.0, rp).astype(BF16))
    qm = jnp.concatenate(rows, axis=0)

    w_t = sm_ref[0].T * ((IDX_DIM ** -0.5) * (IDX_HEADS ** -0.5))

    def score_blk(kb, c):
        lg = _nt_dot(ki_scr[kb], qm)
        s_scr[kb] = _nt_dot(ka_scr[kb], qa)
        sc = jnp.zeros((kblk, LANES), F32)
        for h in range(IDX_HEADS):
            sc = sc + jnp.maximum(lg[:, h * BLK:(h + 1) * BLK], 0.0) * w_t[SM_IW + h:SM_IW + h + 1, :]
        sc = jnp.where(sc == 0.0, 0.0, sc)
        bits = pltpu.bitcast(sc, I32)
        key = bits ^ ((bits >> 31) & 0x7FFFFFFF)
        causal = (kb * kblk + krow) <= (j * BLK + klane)
        key_scr[kb] = jnp.where(causal, key, KEY_NEG_INF)
        return c
    lax.fori_loop(0, nk, score_blk, 0)

    def count(pred):
        def body(kb, acc):
            idx = kb * kblk + krow
            m = pred(key_scr[kb], idx).astype(I32)
            return acc + m.reshape(kblk // 8, 8, LANES).sum(axis=0)
        acc = lax.fori_loop(0, nk, body, jnp.zeros((8, LANES), I32))
        return acc.sum(axis=0, keepdims=True)

    def search():
        def bit_step(i, tu):
            cand = tu | lax.shift_left(jnp.int32(1), 31 - i)
            cs = cand ^ INT_MIN
            cnt = count(lambda k, idx: k >= cs)
            return jnp.where(cnt >= n_sel, cand, tu)
        tu = lax.fori_loop(0, 32, bit_step, jnp.zeros((1, LANES), I32))
        t = tu ^ INT_MIN
        need = n_sel - count(lambda k, idx: k > t)
        n_eq = count(lambda k, idx: k == t)

        def tie_break():
            nbits = max(1, (seq - 1).bit_length())

            def jbit(i, j0):
                cand = j0 | lax.shift_left(jnp.int32(1), nbits - 1 - i)
                g = count(lambda k, idx: (k == t) & (idx < cand))
                return jnp.where(g < need, cand, j0)
            return lax.fori_loop(0, nbits, jbit, jnp.zeros((1, LANES), I32))

        j0 = lax.cond(jnp.all(n_eq == need), lambda: jnp.full((1, LANES), seq, I32), tie_break)
        return t, j0

    def take_all():
        return jnp.full((1, LANES), KEY_NEG_INF, I32), jnp.full((1, LANES), -1, I32)

    t, j0 = lax.cond((j + 1) * BLK > n_sel, search, take_all)

    def mask_blk(kb, macc):
        key = key_scr[kb]
        idx = kb * kblk + krow
        sel = (key > t) | ((key == t) & (idx <= j0))
        s = s_scr[kb]
        sm = jnp.concatenate(
            [jnp.where(sel, s[:, h * BLK:(h + 1) * BLK], -jnp.inf) for h in range(A_HEADS)], axis=1)
        s_scr[kb] = sm
        return jnp.maximum(macc, sm.reshape(kblk // 8, 8, A_HEADS * BLK).max(axis=0))
    macc = lax.fori_loop(0, nk, mask_blk, jnp.full((8, A_HEADS * BLK), -jnp.inf, F32))
    m = macc.max(axis=0, keepdims=True)

    acc_scr[...] = jnp.zeros_like(acc_scr)

    def pv_blk(kb, lacc):
        p = jnp.exp(s_scr[kb] - m)
        acc_scr[...] += _dot(vt_scr[kb], p.astype(BF16))
        return lacc + p.reshape(kblk // 8, 8, A_HEADS * BLK).sum(axis=0)
    lacc = lax.fori_loop(0, nk, pv_blk, jnp.zeros((8, A_HEADS * BLK), F32))
    inv_l = 1.0 / lacc.sum(axis=0, keepdims=True)
    o_t = acc_scr[...] * inv_l
    for h in range(A_HEADS):
        y_ref[0, :, h * 128:(h + 1) * 128] = o_t[:, h * BLK:(h + 1) * BLK].T.astype(y_ref.dtype)


def _dsa_call(p3, sm3, tabs, n_sel):
    bsz, seq, _ = p3.shape
    nqb = seq // BLK
    kblk = min(KBLK, seq)
    nkb = seq // kblk
    cosa, sina, cosi, sini = tabs
    full = lambda b, j: (0, 0)
    qblk = lambda b, j: (j, 0)
    return pl.pallas_call(
        functools.partial(_dsa_body, n_sel),
        grid=(bsz, nqb),
        in_specs=[
            pl.BlockSpec((1, BLK, 512), lambda b, j: (b, j, 0)),
            pl.BlockSpec((1, seq, 512), lambda b, j: (b, 0, 1)),
            pl.BlockSpec((1, BLK, 512), lambda b, j: (b, j, 2)),
            pl.BlockSpec((1, BLK, LANES), lambda b, j: (b, j, 0)),
            pl.BlockSpec((seq, LANES), full), pl.BlockSpec((seq, LANES), full),
            pl.BlockSpec((seq, LANES), full), pl.BlockSpec((seq, LANES), full),
            pl.BlockSpec((BLK, LANES), qblk), pl.BlockSpec((BLK, LANES), qblk),
            pl.BlockSpec((BLK, LANES), qblk), pl.BlockSpec((BLK, LANES), qblk),
        ],
        out_specs=pl.BlockSpec((1, BLK, A_W), lambda b, j: (b, j, 0)),
        out_shape=jax.ShapeDtypeStruct((bsz, seq, A_W), BF16),
        scratch_shapes=[
            pltpu.VMEM((nkb, kblk, 128), BF16),
            pltpu.VMEM((nkb, kblk, 128), BF16),
            pltpu.VMEM((nkb, 128, kblk), BF16),
            pltpu.VMEM((nkb, kblk, LANES), I32),
            pltpu.VMEM((nkb, kblk, A_HEADS * BLK), F32),
            pltpu.VMEM((128, A_HEADS * BLK), F32),
        ],
        compiler_params=pltpu.CompilerParams(
            dimension_semantics=("arbitrary", "arbitrary"), vmem_limit_bytes=VMEM_LIMIT),
        name="dsa",
    )(p3, p3, p3, sm3, cosa, sina, cosi, sini, cosa, sina, cosi, sini)


def _gdn_body(q_ref, k_ref, v_ref, z_ref, sm_ref, cw_ref, alog_ref, dtb_ref, gn_ref,
              y_ref, s_scr):
    c = GDN_BLK
    hc = GDN_CHUNK
    nc = q_ref.shape[1] // c
    row = lax.broadcasted_iota(I32, (c, LANES), 0)
    col = lax.broadcasted_iota(I32, (c, LANES), 1)

    def same_block(size):
        sh = int(math.log2(size))
        return (row >> sh) == (col >> sh)

    same_chunk = same_block(hc)
    incl = same_chunk & (row >= col)
    strict = same_chunk & (row > col)
    eye = (row == col).astype(F32)
    sizes = [GDN_BASE * 2 ** i for i in range(int(math.log2(hc // GDN_BASE)) + 1)]
    base_mask = same_block(GDN_BASE) & (row > col)
    level_masks = [same_block(big) & jnp.logical_not(same_block(small))
                   for small, big in zip(sizes[:-1], sizes[1:])]
    top_rows = row < hc
    s_scr[...] = jnp.zeros_like(s_scr)
    row_w = lax.broadcasted_iota(I32, (c, B_W), 0)

    def split(a):
        hi = a.astype(BF16)
        return hi, (a - hi.astype(F32)).astype(BF16)

    def mm3(a, b):
        ah, al = a
        bh, bl = b
        return _dot(ah, bh) + (_dot(ah, bl) + _dot(al, bh))

    def unit_lower_inverse(ms):
        n1 = [jnp.where(base_mask, -m, 0.0) for m in ms]
        d = [eye + x for x in n1]
        pw = [split(x) for x in n1]
        for _ in range(int(math.log2(GDN_BASE)) - 1):
            pw = [split(mm3(x, x)) for x in pw]
            d = [x + mm3(split(x), p) for x, p in zip(d, pw)]
        for lm in level_masks:
            ds = [split(x) for x in d]
            t = [split(mm3(y, split(jnp.where(lm, m, 0.0)))) for y, m in zip(ds, ms)]
            d = [x - mm3(tt, y) for x, tt, y in zip(d, t, ds)]
        return d

    def conv_silu(ref, n, wofs):
        r0 = pl.multiple_of(n * c, c)
        cur = ref[0, pl.ds(r0, c), :].astype(F32)
        pr0 = pl.multiple_of(jnp.maximum(n - 1, 0) * c, c)
        prev = ref[0, pl.ds(pr0, c), :].astype(F32) * (n > 0).astype(F32)
        y = cur * cw_ref[CONV_WIDTH - 1:CONV_WIDTH, wofs:wofs + B_W]
        for s in range(1, CONV_WIDTH):
            sh = jnp.where(row_w < s, pltpu.roll(prev, s, 0), pltpu.roll(cur, s, 0))
            y = y + sh * cw_ref[CONV_WIDTH - 1 - s:CONV_WIDTH - s, wofs:wofs + B_W]
        return _silu(y)

    def chunk(n, carry):
        r0 = pl.multiple_of(n * c, c)
        qc = conv_silu(q_ref, n, 0)
        kc = conv_silu(k_ref, n, B_W)
        vc = conv_silu(v_ref, n, 2 * B_W)
        sm = sm_ref[0, pl.ds(r0, c), :]
        g = -jnp.exp(alog_ref[...]) * _softplus(sm + dtb_ref[...])
        beta = _sigmoid(sm)
        s = 1
        while s < hc:
            g = g + jnp.where((row & (hc - 1)) >= s, pltpu.roll(g, s, 0), 0.0)
            s *= 2
        g_t = g.T
        heads = range(B_HEADS)
        sls = [slice(h * 128, (h + 1) * 128) for h in heads]
        gcol = [jnp.broadcast_to(g[:, SM_BA + h:SM_BA + h + 1], (c, LANES)) for h in heads]
        grow = [jnp.broadcast_to(g_t[SM_BA + h:SM_BA + h + 1, :], (c, LANES)) for h in heads]
        bcol = [jnp.broadcast_to(beta[:, SM_BB + h:SM_BB + h + 1], (c, LANES)) for h in heads]
        qh = [qc[:, sl] for sl in sls]
        kh = [kc[:, sl] for sl in sls]
        qh = [x * lax.rsqrt(jnp.sum(x * x, axis=-1, keepdims=True) + EPS) * (B_HEAD_DIM ** -0.5) for x in qh]
        kh = [x * lax.rsqrt(jnp.sum(x * x, axis=-1, keepdims=True) + EPS) for x in kh]
        decay = [jnp.exp(jnp.where(incl, gc - gr, -jnp.inf)) for gc, gr in zip(gcol, grow)]
        kbeta = [k * b for k, b in zip(kh, bcol)]
        prod = [_nt_dot(jnp.concatenate([kb.astype(BF16), q.astype(BF16)], axis=0), k.astype(BF16))
                for kb, q, k in zip(kbeta, qh, kh)]
        ms = [jnp.where(strict, p[:c] * dc, 0.0) for p, dc in zip(prod, decay)]
        a_intra = [(p[c:] * dc).astype(BF16) for p, dc in zip(prod, decay)]
        inv = unit_lower_inverse(ms)
        egc = [jnp.exp(x) for x in gcol]
        rhs = [jnp.concatenate([(vc[:, sl] * b).astype(BF16), (kb * e).astype(BF16)], axis=1)
               for sl, b, kb, e in zip(sls, bcol, kbeta, egc)]
        uw = [_dot(i.astype(BF16), r) for i, r in zip(inv, rhs)]
        u = [x[:, :128] for x in uw]
        wb = [x[:, 128:].astype(BF16) for x in uw]
        qg = [(q * e).astype(BF16) for q, e in zip(qh, egc)]
        kd_t = [(k * jnp.exp(jnp.where(top_rows, gc[hc - 1:hc, :], gc[c - 1:c, :]) - gc)).T
                for k, gc in zip(kh, gcol)]
        st = [s_scr[h] for h in heads]
        o_parts = [[] for _ in heads]
        for ci in range(c // hc):
            rs = slice(ci * hc, (ci + 1) * hc)
            in_chunk = (col >> int(math.log2(hc))) == ci
            stb = [x.astype(BF16) for x in st]
            v_new = [u[h][rs] - _dot(wb[h][rs], stb[h]) for h in heads]
            vn_pad = [jnp.concatenate([x] * (c // hc), axis=0).astype(BF16) for x in v_new]
            for h in heads:
                o_parts[h].append(_dot(qg[h][rs], stb[h]) + _dot(a_intra[h][rs], vn_pad[h]))
            st = [st[h] * jnp.exp(gcol[h][(ci + 1) * hc - 1:(ci + 1) * hc, :])
                  + _dot(jnp.where(in_chunk, kd_t[h], 0.0).astype(BF16), vn_pad[h]) for h in heads]
        for h in heads:
            s_scr[h] = st[h]
            o = jnp.concatenate(o_parts[h], axis=0)
            zo = z_ref[0, pl.ds(r0, c), sls[h]].astype(F32)
            on = o * lax.rsqrt(jnp.mean(o * o, axis=-1, keepdims=True) + EPS) * gn_ref[...]
            y_ref[0, pl.ds(r0, c), sls[h]] = (on * _silu(zo)).astype(y_ref.dtype)
        return carry

    lax.fori_loop(0, nc, chunk, 0)


def _gdn_call(p3, sm3, conv_w, alog_v, dtb_v, gnorm):
    bsz, seq, _ = p3.shape
    col = lambda g: (lambda b: (b, 0, g))
    vec = lambda b: (0, 0)
    return pl.pallas_call(
        _gdn_body,
        grid=(bsz,),
        in_specs=[
            pl.BlockSpec((1, seq, B_W), col(3)), pl.BlockSpec((1, seq, B_W), col(4)),
            pl.BlockSpec((1, seq, B_W), col(5)), pl.BlockSpec((1, seq, B_W), col(6)),
            pl.BlockSpec((1, seq, LANES), lambda b: (b, 0, 0)),
            pl.BlockSpec((CONV_WIDTH, 3 * B_W), vec),
            pl.BlockSpec((1, LANES), vec), pl.BlockSpec((1, LANES), vec), pl.BlockSpec((1, LANES), vec),
        ],
        out_specs=pl.BlockSpec((1, seq, B_W), lambda b: (b, 0, 0)),
        out_shape=jax.ShapeDtypeStruct((bsz, seq, B_W), BF16),
        scratch_shapes=[pltpu.VMEM((B_HEADS, 128, 128), F32)],
        compiler_params=pltpu.CompilerParams(
            dimension_semantics=("arbitrary",), vmem_limit_bytes=VMEM_LIMIT),
        name="gdn",
    )(p3, p3, p3, p3, sm3, conv_w, alog_v, dtb_v, gnorm)


def _ret_body(qk_ref, v_ref, g_ref, cos_ref, sin_ref, dmat_ref, xi_ref, zeta_ref, gch_ref,
              y_ref, s_scr):
    c = RET_BLK
    nc = v_ref.shape[1] // c
    lane = lax.broadcasted_iota(I32, (c, LANES), 1)
    first = (lane & 32) == 0
    lo_head = lane < 64
    s_scr[...] = jnp.zeros_like(s_scr)

    def chunk(n, carry):
        r0 = pl.multiple_of(n * c, c)
        qk = qk_ref[0, pl.ds(r0, c), :].astype(F32)
        cs = cos_ref[pl.ds(r0, c), :]
        sn = sin_ref[pl.ds(r0, c), :]
        for p in range(C_HEADS // 2):
            qp = _rot_half32(qk[:, p * 128:(p + 1) * 128], cs, sn, first)
            kp = _rot_half32(qk[:, C_QK + p * 128:C_QK + (p + 1) * 128], cs, sn, first) * (C_KEY_DIM ** -0.5)
            qm = [jnp.where(lo_head, qp, 0.0).astype(BF16), jnp.where(lo_head, 0.0, qp).astype(BF16)]
            km = [jnp.where(lo_head, kp, 0.0), jnp.where(lo_head, 0.0, kp)]
            sc = _nt_dot(jnp.concatenate(qm, axis=0), kp.astype(BF16))
            for e in range(2):
                h = 2 * p + e
                sl = slice(h * 128, (h + 1) * 128)
                vh = v_ref[0, pl.ds(r0, c), sl].astype(F32)
                scores = sc[e * c:(e + 1) * c] * dmat_ref[h]
                st = s_scr[h]
                o = _dot(scores.astype(BF16), vh.astype(BF16)) + _dot(qm[e], st.astype(BF16)) * xi_ref[h]
                s_scr[h] = st * gch_ref[h, 0:1, :] + _dot(km[e].T.astype(BF16), (vh * zeta_ref[h]).astype(BF16))
                on = o * lax.rsqrt(jnp.mean(o * o, axis=-1, keepdims=True) + EPS)
                go = g_ref[0, pl.ds(r0, c), sl].astype(F32)
                y_ref[0, pl.ds(r0, c), sl] = (on * _silu(go)).astype(y_ref.dtype)
        return carry

    lax.fori_loop(0, nc, chunk, 0)


def _ret_call(p3, cos_t, sin_t, dmat, xi, zeta, gch):
    bsz, seq, _ = p3.shape
    c = RET_BLK
    z2 = lambda b: (0, 0)
    z3 = lambda b: (0, 0, 0)
    return pl.pallas_call(
        _ret_body,
        grid=(bsz,),
        in_specs=[
            pl.BlockSpec((1, seq, 512), lambda b: (b, 0, 7)),
            pl.BlockSpec((1, seq, 512), lambda b: (b, 0, 8)),
            pl.BlockSpec((1, seq, 512), lambda b: (b, 0, 9)),
            pl.BlockSpec((seq, LANES), z2), pl.BlockSpec((seq, LANES), z2),
            pl.BlockSpec((C_HEADS, c, c), z3),
            pl.BlockSpec((C_HEADS, c, LANES), z3), pl.BlockSpec((C_HEADS, c, LANES), z3),
            pl.BlockSpec((C_HEADS, 8, LANES), z3),
        ],
        out_specs=pl.BlockSpec((1, seq, C_W), lambda b: (b, 0, 0)),
        out_shape=jax.ShapeDtypeStruct((bsz, seq, C_W), BF16),
        scratch_shapes=[pltpu.VMEM((C_HEADS, 128, 128), F32)],
        compiler_params=pltpu.CompilerParams(
            dimension_semantics=("arbitrary",), vmem_limit_bytes=VMEM_LIMIT),
        name="ret",
    )(p3, p3, p3, cos_t, sin_t, dmat, xi, zeta, gch)


def _merge_body(ya_ref, yb_ref, yc_ref, g0_ref, g1_ref, g2_ref, x_ref, wb_ref, wo_ref, ln_ref, o_ref):
    def gate(ref):
        return _sigmoid(ref[...].astype(F32))
    merged = (gate(g0_ref) * _dot(ya_ref[...], wb_ref[0:A_W, :])
              + gate(g1_ref) * _dot(yb_ref[...], wb_ref[A_W:A_W + B_W, :])
              + gate(g2_ref) * _dot(yc_ref[...], wb_ref[A_W + B_W:, :]))
    o = _dot(merged.astype(BF16), wo_ref[...])
    on = o * lax.rsqrt(jnp.mean(o * o, axis=-1, keepdims=True) + EPS) * ln_ref[...]
    o_ref[...] = x_ref[...] + on


def _merge_call(ya, yb, yc, p2d, x2d, wb, wo, ln):
    n = x2d.shape[0]
    tm = min(512, n)
    rowb = lambda i: (i, 0)
    cst = lambda i: (0, 0)
    gcol = lambda g: (lambda i: (i, g))
    return pl.pallas_call(
        _merge_body,
        grid=(n // tm,),
        in_specs=[
            pl.BlockSpec((tm, A_W), rowb), pl.BlockSpec((tm, B_W), rowb), pl.BlockSpec((tm, C_W), rowb),
            pl.BlockSpec((tm, D_MODEL), gcol(5)), pl.BlockSpec((tm, D_MODEL), gcol(6)),
            pl.BlockSpec((tm, D_MODEL), gcol(7)),
            pl.BlockSpec((tm, D_MODEL), rowb),
            pl.BlockSpec((A_W + B_W + C_W, D_MODEL), cst),
            pl.BlockSpec((D_MODEL, D_MODEL), cst),
            pl.BlockSpec((1, D_MODEL), cst),
        ],
        out_specs=pl.BlockSpec((tm, D_MODEL), rowb),
        out_shape=jax.ShapeDtypeStruct((n, D_MODEL), F32),
        compiler_params=pltpu.CompilerParams(
            dimension_semantics=("arbitrary",), vmem_limit_bytes=VMEM_LIMIT),
        name="merge",
    )(ya, yb, yc, p2d, p2d, p2d, x2d, wb, wo, ln)


def _mlp_body(x_ref, lnpre_ref, wu_ref, wd_ref, lnpost_ref, o_ref):
    x = x_ref[...]
    h = (x * lax.rsqrt(jnp.mean(x * x, axis=-1, keepdims=True) + EPS) * lnpre_ref[...]).astype(BF16)
    ff = jnp.zeros(x.shape, F32)
    for cidx in range(D_FF // D_MODEL):
        sl = slice(cidx * D_MODEL, (cidx + 1) * D_MODEL)
        u = jnp.maximum(_dot(h, wu_ref[:, sl]), 0.0)
        ff = ff + _dot((u * u).astype(BF16), wd_ref[sl, :])
    o_ref[...] = x + ff * lax.rsqrt(jnp.mean(ff * ff, axis=-1, keepdims=True) + EPS) * lnpost_ref[...]


def _mlp_call(x2d, lnpre, wu, wd, lnpost):
    n = x2d.shape[0]
    tm = min(512, n)
    rowb = lambda i: (i, 0)
    cst = lambda i: (0, 0)
    return pl.pallas_call(
        _mlp_body,
        grid=(n // tm,),
        in_specs=[
            pl.BlockSpec((tm, D_MODEL), rowb),
            pl.BlockSpec((1, D_MODEL), cst),
            pl.BlockSpec((D_MODEL, D_FF), cst),
            pl.BlockSpec((D_FF, D_MODEL), cst),
            pl.BlockSpec((1, D_MODEL), cst),
        ],
        out_specs=pl.BlockSpec((tm, D_MODEL), rowb),
        out_shape=jax.ShapeDtypeStruct((n, D_MODEL), F32),
        compiler_params=pltpu.CompilerParams(
            dimension_semantics=("arbitrary",), vmem_limit_bytes=VMEM_LIMIT),
        name="mlp",
    )(x2d, lnpre, wu, wd, lnpost)


def _rope_tables(seq):
    pos = jnp.arange(seq, dtype=F32)
    inv_a = ROPE_THETA ** (-jnp.arange(0, A_HEAD_DIM, 2, dtype=F32) / A_HEAD_DIM)
    ang_a = pos[:, None] * inv_a[None, :]
    cosa = jnp.concatenate([jnp.cos(ang_a)] * 2, axis=1)
    sina = jnp.concatenate([-jnp.sin(ang_a), jnp.sin(ang_a)], axis=1)
    inv_i = ROPE_THETA ** (-jnp.arange(0, IDX_DIM, 2, dtype=F32) / IDX_DIM)
    ang_i = pos[:, None] * inv_i[None, :]
    cosi = jnp.concatenate([jnp.cos(ang_i)] * 4, axis=1)
    sini = jnp.concatenate([-jnp.sin(ang_i), jnp.sin(ang_i)] * 2, axis=1)
    inv_c = 1.0 / (ROPE_THETA ** jnp.linspace(0.0, 1.0, C_KEY_DIM // 2, dtype=F32))
    ang_c = pos[:, None] * inv_c[None, :]
    cosc = jnp.concatenate([jnp.cos(ang_c)] * 4, axis=1)
    sinc = jnp.concatenate([-jnp.sin(ang_c), jnp.sin(ang_c)] * 2, axis=1)
    return (cosa, sina, cosi, sini), (cosc, sinc)


def _ret_tables():
    c = RET_BLK
    log_gamma = jnp.log(1.0 - 2.0 ** (-5.0 - jnp.arange(C_HEADS, dtype=F32)))
    r = jnp.arange(c, dtype=F32)
    rel = r[:, None] - r[None, :]
    dmat = jnp.where(rel >= 0, jnp.exp(jnp.maximum(rel, 0.0)[None] * log_gamma[:, None, None]), 0.0)
    xi = jnp.exp((r + 1.0)[None] * log_gamma[:, None])
    zeta = jnp.exp((c - 1.0 - r)[None] * log_gamma[:, None])
    gch = jnp.exp(c * log_gamma)
    xi_b = jnp.broadcast_to(xi[:, :, None], (C_HEADS, c, LANES))
    zeta_b = jnp.broadcast_to(zeta[:, :, None], (C_HEADS, c, LANES))
    gch_b = jnp.broadcast_to(gch[:, None, None], (C_HEADS, 8, LANES))
    return dmat, xi_b, zeta_b, gch_b


def _lane_vec(vals, ofs):
    v = jnp.zeros((1, LANES), F32)
    return v.at[0, ofs:ofs + vals.shape[0]].set(vals.astype(F32))


def kernel(x, ln_mix_pre, w_in, gdn_conv, gdn_a_log, gdn_dt_bias, gdn_norm, w_branch, w_out,
           ln_mix_post, ln_mlp_pre, w_up, w_down, ln_mlp_post):
    bsz, seq, _ = x.shape
    depth = w_in.shape[0]
    n = bsz * seq
    n_sel = min(TOPK_MAX, seq // 4)
    assert seq % BLK == 0 and n_sel % BLK == 0
    dsa_tabs, (cosc, sinc) = _rope_tables(seq)
    dmat, xi_b, zeta_b, gch_b = _ret_tables()

    x2d = x.reshape(n, D_MODEL)
    for l in range(depth):
        w_big, w_small = _prep_w_in(w_in[l])
        p2d, sm2d = _proj_call(x2d, ln_mix_pre[l][None, :], w_big, w_small)
        p3 = p2d.reshape(bsz, seq, P_WIDTH)
        sm3 = sm2d.reshape(bsz, seq, LANES)
        y_a = _dsa_call(p3, sm3, dsa_tabs, n_sel)
        y_b = _gdn_call(p3, sm3, gdn_conv[l], _lane_vec(gdn_a_log[l], SM_BA),
                        _lane_vec(gdn_dt_bias[l], SM_BA), gdn_norm[l][None, :])
        y_c = _ret_call(p3, cosc, sinc, dmat, xi_b, zeta_b, gch_b)
        x2d = _merge_call(y_a.reshape(n, A_W), y_b.reshape(n, B_W), y_c.reshape(n, C_W), p2d, x2d,
                          w_branch[l].astype(BF16), w_out[l].astype(BF16), ln_mix_post[l][None, :])
        x2d = _mlp_call(x2d, ln_mlp_pre[l][None, :], w_up[l].astype(BF16), w_down[l].astype(BF16),
                        ln_mlp_post[l][None, :])
    return x2d.reshape(bsz, seq, D_MODEL)
```

```python
import functools
import math

import numpy as np
import jax
import jax.numpy as jnp
from jax import lax
from jax.experimental import pallas as pl
from jax.experimental.pallas import tpu as pltpu

F32 = jnp.float32
BF16 = jnp.bfloat16
I32 = jnp.int32

D_MODEL = 1024
A_HEADS = 4
A_HEAD_DIM = 128
IDX_HEADS = 8
IDX_DIM = 64
TOPK_MAX = 256
B_HEADS = 4
B_HEAD_DIM = 128
CONV_WIDTH = 4
C_HEADS = 4
C_KEY_DIM = 64
C_VAL_DIM = 128
D_FF = 4 * D_MODEL
ROPE_THETA = 10000.0
EPS = 1e-6
N_BRANCH = 3
A_W = A_HEADS * A_HEAD_DIM
B_W = B_HEADS * B_HEAD_DIM
C_W = C_HEADS * C_VAL_DIM
C_QK = C_HEADS * C_KEY_DIM

LANES = 128
BLK = 128
KBLK = 256
GDN_BLK = 128
GDN_CHUNK = 64
GDN_BASE = 8
RET_BLK = 128
VMEM_LIMIT = 52 * 1024 * 1024

_OFF = {}
_o = 0
for _name, _w in (("a_q", A_W), ("a_k", A_HEAD_DIM), ("a_v", A_HEAD_DIM),
                  ("i_q", IDX_HEADS * IDX_DIM), ("i_k", IDX_DIM), ("i_w", IDX_HEADS),
                  ("b_q", B_W), ("b_k", B_W), ("b_v", B_W), ("b_a", B_HEADS), ("b_b", B_HEADS),
                  ("b_z", B_W), ("c_q", C_QK), ("c_k", C_QK), ("c_v", C_W), ("c_g", C_W),
                  ("gate", N_BRANCH * D_MODEL)):
    _OFF[_name] = (_o, _w)
    _o += _w

P_WIDTH = 8192
P_TILE = 1024
SM_IW = 0
SM_BA = 8
SM_BB = 12

KEY_NEG_INF = -2139095041
INT_MIN = -2147483648
HI16 = -65536
MIN_NORMAL_BITS = 0x00800000


def _cols(w, name):
    o, n = _OFF[name]
    return w[:, o:o + n]


def _prep_w_in(w):
    ik = _cols(w, "i_k")
    big = jnp.concatenate([
        _cols(w, "a_q"), _cols(w, "a_k"), _cols(w, "a_v"), ik, ik, ik, ik,
        _cols(w, "i_q"),
        _cols(w, "b_q"), _cols(w, "b_k"), _cols(w, "b_v"), _cols(w, "b_z"),
        _cols(w, "c_q"), _cols(w, "c_k"), _cols(w, "c_v"), _cols(w, "c_g"),
        _cols(w, "gate")], axis=1)
    assert big.shape[1] == P_WIDTH
    small = jnp.concatenate([
        _cols(w, "i_w"), _cols(w, "b_a"), _cols(w, "b_b"),
        jnp.zeros((w.shape[0], LANES - 16), w.dtype)], axis=1)
    return big.astype(BF16), small.astype(BF16)


def _nt_dot(a, b):
    return lax.dot_general(a, b, (((1,), (1,)), ((), ())), preferred_element_type=F32)


def _dot(a, b):
    return jnp.dot(a, b, preferred_element_type=F32)


def _sigmoid(x):
    return 0.5 + 0.5 * jnp.tanh(0.5 * x)


def _silu(x):
    return x * _sigmoid(x)


def _softplus(x):
    return jnp.maximum(x, 0.0) + jnp.log(1.0 + jnp.exp(-jnp.abs(x)))


def _rot_half64(x, cos, sin_signed):
    return x * cos + pltpu.roll(x, 64, 1) * sin_signed


def _rot_half32(x, cos, sin_signed, first):
    partner = jnp.where(first, pltpu.roll(x, 96, 1), pltpu.roll(x, 32, 1))
    return x * cos + partner * sin_signed


def _proj_body(x_ref, gain_ref, w_ref, ws_ref, p_ref, s_ref):
    x = x_ref[...]
    ms = jnp.mean(x * x, axis=-1, keepdims=True)
    h = (x * lax.rsqrt(ms + EPS) * gain_ref[...]).astype(BF16)
    s_ref[...] = _dot(h, ws_ref[...])
    for t in range(P_WIDTH // P_TILE):
        sl = slice(t * P_TILE, (t + 1) * P_TILE)
        p_ref[:, sl] = _dot(h, w_ref[:, sl]).astype(p_ref.dtype)


def _proj_call(x2d, gain, w_big, w_small):
    n = x2d.shape[0]
    tm = min(512, n)
    once = pl.Buffered(1)
    return pl.pallas_call(
        _proj_body,
        grid=(n // tm,),
        in_specs=[
            pl.BlockSpec((tm, D_MODEL), lambda i: (i, 0)),
            pl.BlockSpec((1, D_MODEL), lambda i: (0, 0)),
            pl.BlockSpec((D_MODEL, P_WIDTH), lambda i: (0, 0), pipeline_mode=once),
            pl.BlockSpec((D_MODEL, LANES), lambda i: (0, 0), pipeline_mode=once),
        ],
        out_specs=[
            pl.BlockSpec((tm, P_WIDTH), lambda i: (i, 0)),
            pl.BlockSpec((tm, LANES), lambda i: (i, 0)),
        ],
        out_shape=[
            jax.ShapeDtypeStruct((n, P_WIDTH), BF16),
            jax.ShapeDtypeStruct((n, LANES), F32),
        ],
        compiler_params=pltpu.CompilerParams(
            dimension_semantics=("arbitrary",), vmem_limit_bytes=VMEM_LIMIT),
        name="proj",
    )(x2d, gain, w_big, w_small)


def _dsa_body(n_sel, aq_ref, kv_ref, iq_ref, sm_ref,
              cosa_ref, sina_ref, cosi_ref, sini_ref,
              cosaq_ref, sinaq_ref, cosiq_ref, siniq_ref,
              y_ref, ka_scr, ki_scr, vt_scr, key_scr, khi_scr, s_scr, acc_scr):
    j = pl.program_id(1)
    nkb = ka_scr.shape[0]
    kblk = ka_scr.shape[1]
    seq = nkb * kblk
    nk = (j * BLK + BLK + kblk - 1) // kblk
    lane = lax.broadcasted_iota(I32, (BLK, LANES), 1)
    klane = lax.broadcasted_iota(I32, (kblk, LANES), 1)
    krow = lax.broadcasted_iota(I32, (kblk, LANES), 0)
    first = (lane & 32) == 0
    kfirst = (klane & 32) == 0
    lo_head = lane < 64

    @pl.when(j == 0)
    def _():
        def prep(kb, c):
            r0 = pl.multiple_of(kb * kblk, kblk)
            kv = kv_ref[0, pl.ds(r0, kblk), :].astype(F32)
            ca = cosa_ref[pl.ds(r0, kblk), :]
            sa = sina_ref[pl.ds(r0, kblk), :]
            ci = cosi_ref[pl.ds(r0, kblk), :]
            si = sini_ref[pl.ds(r0, kblk), :]
            ka_scr[kb] = _rot_half64(kv[:, 0:128], ca, sa).astype(BF16)
            ki_scr[kb] = _rot_half32(kv[:, 256:384], ci, si, kfirst).astype(BF16)
            vt_scr[kb] = kv[:, 128:256].T.astype(BF16)
            return c
        lax.fori_loop(0, nkb, prep, 0)

    aq = aq_ref[0].astype(F32)
    caq = cosaq_ref[...]
    saq = sinaq_ref[...]
    qa = jnp.concatenate(
        [(_rot_half64(aq[:, h * 128:(h + 1) * 128], caq, saq) * (A_HEAD_DIM ** -0.5)).astype(BF16)
         for h in range(A_HEADS)], axis=0)

    iq = iq_ref[0].astype(F32)
    ciq = cosiq_ref[...]
    siq = siniq_ref[...]
    rows = []
    for p in range(IDX_HEADS // 2):
        rp = _rot_half32(iq[:, p * 128:(p + 1) * 128], ciq, siq, first)
        rows.append(jnp.where(lo_head, rp, 0.0).astype(BF16))
        rows.append(jnp.where(lo_head, 0.0, rp).astype(BF16))
    qm = jnp.concatenate(rows, axis=0)

    w_t = sm_ref[0].T * ((IDX_DIM ** -0.5) * (IDX_HEADS ** -0.5))

    def score_blk(kb, c):
        lg = _nt_dot(ki_scr[kb], qm)
        s_scr[kb] = _nt_dot(ka_scr[kb], qa)
        sc = jnp.zeros((kblk, LANES), F32)
        for h in range(IDX_HEADS):
            sc = sc + jnp.maximum(lg[:, h * BLK:(h + 1) * BLK], 0.0) * w_t[SM_IW + h:SM_IW + h + 1, :]
        sc = jnp.where(sc == 0.0, 0.0, sc)
        bits = pltpu.bitcast(sc, I32)
        key = bits ^ ((bits >> 31) & 0x7FFFFFFF)
        causal = (kb * kblk + krow) <= (j * BLK + klane)
        key_scr[kb] = jnp.where(causal, key, KEY_NEG_INF)
        hi = pltpu.bitcast(bits & HI16, F32)
        khi_scr[kb] = jnp.where(causal, hi, -jnp.inf).astype(BF16)
        return c
    lax.fori_loop(0, nk, score_blk, 0)

    @pl.when(nk % 2 == 1)
    def _():
        key_scr[nk] = jnp.full((kblk, LANES), KEY_NEG_INF, I32)
        khi_scr[nk] = jnp.full((kblk, LANES), -jnp.inf, BF16)
    nk2 = (nk + 1) // 2

    def count_hi(cf):
        one = jnp.ones((), BF16)
        zero = jnp.zeros((), BF16)

        def body(kp, acc):
            parts = []
            for kb in (2 * kp, 2 * kp + 1):
                m = jnp.where(khi_scr[kb] >= cf, one, zero)
                parts += [m[r * 16:(r + 1) * 16, :] for r in range(kblk // 16)]
            while len(parts) > 1:
                parts = [a + b for a, b in zip(parts[0::2], parts[1::2])]
            return acc + parts[0]
        acc = lax.fori_loop(0, nk2, body, jnp.zeros((16, LANES), BF16))
        return acc.astype(F32).sum(axis=0, keepdims=True)

    def count(pred):
        def body(kp, acc):
            for kb in (2 * kp, 2 * kp + 1):
                m = pred(key_scr[kb], kb * kblk + krow).astype(I32)
                acc = acc + m.reshape(kblk // 8, 8, LANES).sum(axis=0)
            return acc
        acc = lax.fori_loop(0, nk2, body, jnp.zeros((8, LANES), I32))
        return acc.sum(axis=0, keepdims=True)

    def search():
        def hi_step(i, tu):
            cand = tu | lax.shift_left(jnp.int32(1), 31 - i)
            cs = cand ^ INT_MIN
            fbits = (cs ^ ((cs >> 31) & 0x7FFFFFFF)) & HI16
            fbits = jnp.where((fbits > 0) & (fbits < MIN_NORMAL_BITS), MIN_NORMAL_BITS, fbits)
            cnt = count_hi(pltpu.bitcast(fbits, F32).astype(BF16))
            return jnp.where(cnt >= n_sel, cand, tu)

        def bit_step(i, tu):
            cand = tu | lax.shift_left(jnp.int32(1), 31 - i)
            cs = cand ^ INT_MIN
            cnt = count(lambda k, idx: k >= cs)
            return jnp.where(cnt >= n_sel, cand, tu)
        tu = lax.fori_loop(0, 16, hi_step, jnp.zeros((1, LANES), I32))
        tu = lax.fori_loop(16, 32, bit_step, tu)
        t = tu ^ INT_MIN
        need = n_sel - count(lambda k, idx: k > t)
        n_eq = count(lambda k, idx: k == t)

        def tie_break():
            nbits = max(1, (seq - 1).bit_length())

            def jbit(i, j0):
                cand = j0 | lax.shift_left(jnp.int32(1), nbits - 1 - i)
                g = count(lambda k, idx: (k == t) & (idx < cand))
                return jnp.where(g < need, cand, j0)
            return lax.fori_loop(0, nbits, jbit, jnp.zeros((1, LANES), I32))

        j0 = lax.cond(jnp.all(n_eq == need), lambda: jnp.full((1, LANES), seq, I32), tie_break)
        return t, j0

    def take_all():
        return jnp.full((1, LANES), KEY_NEG_INF, I32), jnp.full((1, LANES), -1, I32)

    t, j0 = lax.cond((j + 1) * BLK > n_sel, search, take_all)

    def mask_blk(kb, macc):
        key = key_scr[kb]
        idx = kb * kblk + krow
        sel = (key > t) | ((key == t) & (idx <= j0))
        s = s_scr[kb]
        sm = jnp.concatenate(
            [jnp.where(sel, s[:, h * BLK:(h + 1) * BLK], -jnp.inf) for h in range(A_HEADS)], axis=1)
        s_scr[kb] = sm
        return jnp.maximum(macc, sm.reshape(kblk // 8, 8, A_HEADS * BLK).max(axis=0))
    macc = lax.fori_loop(0, nk, mask_blk, jnp.full((8, A_HEADS * BLK), -jnp.inf, F32))
    m = macc.max(axis=0, keepdims=True)

    acc_scr[...] = jnp.zeros_like(acc_scr)

    def pv_blk(kb, lacc):
        p = jnp.exp(s_scr[kb] - m)
        acc_scr[...] += _dot(vt_scr[kb], p.astype(BF16))
        return lacc + p.reshape(kblk // 8, 8, A_HEADS * BLK).sum(axis=0)
    lacc = lax.fori_loop(0, nk, pv_blk, jnp.zeros((8, A_HEADS * BLK), F32))
    inv_l = 1.0 / lacc.sum(axis=0, keepdims=True)
    o_t = acc_scr[...] * inv_l
    for h in range(A_HEADS):
        y_ref[0, :, h * 128:(h + 1) * 128] = o_t[:, h * BLK:(h + 1) * BLK].T.astype(y_ref.dtype)


def _dsa_call(p3, sm3, tabs, n_sel):
    bsz, seq, _ = p3.shape
    nqb = seq // BLK
    kblk = min(KBLK, seq)
    nkb = seq // kblk
    cosa, sina, cosi, sini = tabs
    full = lambda b, j: (0, 0)
    qblk = lambda b, j: (j, 0)
    return pl.pallas_call(
        functools.partial(_dsa_body, n_sel),
        grid=(bsz, nqb),
        in_specs=[
            pl.BlockSpec((1, BLK, 512), lambda b, j: (b, j, 0)),
            pl.BlockSpec((1, seq, 512), lambda b, j: (b, 0, 1)),
            pl.BlockSpec((1, BLK, 512), lambda b, j: (b, j, 2)),
            pl.BlockSpec((1, BLK, LANES), lambda b, j: (b, j, 0)),
            pl.BlockSpec((seq, LANES), full), pl.BlockSpec((seq, LANES), full),
            pl.BlockSpec((seq, LANES), full), pl.BlockSpec((seq, LANES), full),
            pl.BlockSpec((BLK, LANES), qblk), pl.BlockSpec((BLK, LANES), qblk),
            pl.BlockSpec((BLK, LANES), qblk), pl.BlockSpec((BLK, LANES), qblk),
        ],
        out_specs=pl.BlockSpec((1, BLK, A_W), lambda b, j: (b, j, 0)),
        out_shape=jax.ShapeDtypeStruct((bsz, seq, A_W), BF16),
        scratch_shapes=[
            pltpu.VMEM((nkb, kblk, 128), BF16),
            pltpu.VMEM((nkb, kblk, 128), BF16),
            pltpu.VMEM((nkb, 128, kblk), BF16),
            pltpu.VMEM((nkb, kblk, LANES), I32),
            pltpu.VMEM((nkb, kblk, LANES), BF16),
            pltpu.VMEM((nkb, kblk, A_HEADS * BLK), F32),
            pltpu.VMEM((128, A_HEADS * BLK), F32),
        ],
        compiler_params=pltpu.CompilerParams(
            dimension_semantics=("arbitrary", "arbitrary"), vmem_limit_bytes=VMEM_LIMIT),
        name="dsa",
    )(p3, p3, p3, sm3, cosa, sina, cosi, sini, cosa, sina, cosi, sini)


def _gdn_body(q_ref, k_ref, v_ref, z_ref, sm_ref, cw_ref, alog_ref, dtb_ref, gn_ref,
              y_ref, s_scr):
    c = GDN_BLK
    hc = GDN_CHUNK
    nc = q_ref.shape[1] // c
    row = lax.broadcasted_iota(I32, (c, LANES), 0)
    col = lax.broadcasted_iota(I32, (c, LANES), 1)

    def same_block(size):
        sh = int(math.log2(size))
        return (row >> sh) == (col >> sh)

    same_chunk = same_block(hc)
    incl = same_chunk & (row >= col)
    strict = same_chunk & (row > col)
    eye = (row == col).astype(F32)
    sizes = [GDN_BASE * 2 ** i for i in range(int(math.log2(hc // GDN_BASE)) + 1)]
    base_mask = same_block(GDN_BASE) & (row > col)
    level_masks = [same_block(big) & jnp.logical_not(same_block(small))
                   for small, big in zip(sizes[:-1], sizes[1:])]
    top_rows = row < hc
    s_scr[...] = jnp.zeros_like(s_scr)
    row8 = lax.broadcasted_iota(I32, (8, B_W), 0)

    def split(a):
        hi = a.astype(BF16)
        return hi, (a - hi.astype(F32)).astype(BF16)

    def mm3(a, b):
        ah, al = a
        bh, bl = b
        return _dot(ah, bh) + (_dot(ah, bl) + _dot(al, bh))

    def unit_lower_inverse(ms):
        n1 = [jnp.where(base_mask, -m, 0.0) for m in ms]
        d = [eye + x for x in n1]
        pw = [split(x) for x in n1]
        for _ in range(int(math.log2(GDN_BASE)) - 1):
            pw = [split(mm3(x, x)) for x in pw]
            d = [x + mm3(split(x), p) for x, p in zip(d, pw)]
        for lm in level_masks:
            ds = [split(x) for x in d]
            t = [split(mm3(y, split(jnp.where(lm, m, 0.0)))) for y, m in zip(ds, ms)]
            d = [x - mm3(tt, y) for x, tt, y in zip(d, t, ds)]
        return d

    def conv_silu(ref, n, wofs):
        r0 = pl.multiple_of(n * c, c)
        cur = ref[0, pl.ds(r0, c), :].astype(F32)
        pr0 = pl.multiple_of(jnp.maximum(n, 1) * c - 16, 16)
        tail = ref[0, pl.ds(pr0, 16), :].astype(F32) * (n > 0).astype(F32)
        y = cur * cw_ref[CONV_WIDTH - 1:CONV_WIDTH, wofs:wofs + B_W]
        for s in range(1, CONV_WIDTH):
            sh = pltpu.roll(cur, s, 0)
            top = jnp.where(row8 < s, pltpu.roll(tail, s, 0)[:8], sh[:8])
            sh = jnp.concatenate([top, sh[8:]], axis=0)
            y = y + sh * cw_ref[CONV_WIDTH - 1 - s:CONV_WIDTH - s, wofs:wofs + B_W]
        return _silu(y)

    def chunk(n, carry):
        r0 = pl.multiple_of(n * c, c)
        qc = conv_silu(q_ref, n, 0)
        kc = conv_silu(k_ref, n, B_W)
        vc = conv_silu(v_ref, n, 2 * B_W)
        sm = sm_ref[0, pl.ds(r0, c), :]
        g = -jnp.exp(alog_ref[...]) * _softplus(sm + dtb_ref[...])
        beta = _sigmoid(sm)
        s = 1
        while s < hc:
            g = g + jnp.where((row & (hc - 1)) >= s, pltpu.roll(g, s, 0), 0.0)
            s *= 2
        g_t = g.T
        heads = range(B_HEADS)
        sls = [slice(h * 128, (h + 1) * 128) for h in heads]
        gcol = [jnp.broadcast_to(g[:, SM_BA + h:SM_BA + h + 1], (c, LANES)) for h in heads]
        grow = [jnp.broadcast_to(g_t[SM_BA + h:SM_BA + h + 1, :], (c, LANES)) for h in heads]
        bcol = [jnp.broadcast_to(beta[:, SM_BB + h:SM_BB + h + 1], (c, LANES)) for h in heads]
        qh = [qc[:, sl] for sl in sls]
        kh = [kc[:, sl] for sl in sls]
        qh = [x * lax.rsqrt(jnp.sum(x * x, axis=-1, keepdims=True) + EPS) * (B_HEAD_DIM ** -0.5) for x in qh]
        kh = [x * lax.rsqrt(jnp.sum(x * x, axis=-1, keepdims=True) + EPS) for x in kh]
        decay = [jnp.exp(jnp.where(incl, gc - gr, -jnp.inf)) for gc, gr in zip(gcol, grow)]
        kbeta = [k * b for k, b in zip(kh, bcol)]
        prod = [_nt_dot(jnp.concatenate([kb.astype(BF16), q.astype(BF16)], axis=0), k.astype(BF16))
                for kb, q, k in zip(kbeta, qh, kh)]
        ms = [jnp.where(strict, p[:c] * dc, 0.0) for p, dc in zip(prod, decay)]
        a_intra = [(p[c:] * dc).astype(BF16) for p, dc in zip(prod, decay)]
        inv = unit_lower_inverse(ms)
        egc = [jnp.exp(x) for x in gcol]
        rhs = [jnp.concatenate([(vc[:, sl] * b).astype(BF16), (kb * e).astype(BF16)], axis=1)
               for sl, b, kb, e in zip(sls, bcol, kbeta, egc)]
        uw = [_dot(i.astype(BF16), r) for i, r in zip(inv, rhs)]
        u = [x[:, :128] for x in uw]
        wb = [x[:, 128:].astype(BF16) for x in uw]
        qg = [(q * e).astype(BF16) for q, e in zip(qh, egc)]
        kd_t = [(k * jnp.exp(jnp.where(top_rows, gc[hc - 1:hc, :], gc[c - 1:c, :]) - gc)).T
                for k, gc in zip(kh, gcol)]
        st = [s_scr[h] for h in heads]
        o_parts = [[] for _ in heads]
        for ci in range(c // hc):
            rs = slice(ci * hc, (ci + 1) * hc)
            in_chunk = (col >> int(math.log2(hc))) == ci
            stb = [x.astype(BF16) for x in st]
            v_new = [u[h][rs] - _dot(wb[h][rs], stb[h]) for h in heads]
            vn_pad = [jnp.concatenate([x] * (c // hc), axis=0).astype(BF16) for x in v_new]
            for h in heads:
                o_parts[h].append(_dot(qg[h][rs], stb[h]) + _dot(a_intra[h][rs], vn_pad[h]))
            st = [st[h] * jnp.exp(gcol[h][(ci + 1) * hc - 1:(ci + 1) * hc, :])
                  + _dot(jnp.where(in_chunk, kd_t[h], 0.0).astype(BF16), vn_pad[h]) for h in heads]
        for h in heads:
            s_scr[h] = st[h]
            o = jnp.concatenate(o_parts[h], axis=0)
            zo = z_ref[0, pl.ds(r0, c), sls[h]].astype(F32)
            on = o * lax.rsqrt(jnp.mean(o * o, axis=-1, keepdims=True) + EPS) * gn_ref[...]
            y_ref[0, pl.ds(r0, c), sls[h]] = (on * _silu(zo)).astype(y_ref.dtype)
        return carry

    lax.fori_loop(0, nc, chunk, 0)


def _gdn_call(p3, sm3, conv_w, alog_v, dtb_v, gnorm):
    bsz, seq, _ = p3.shape
    col = lambda g: (lambda b: (b, 0, g))
    vec = lambda b: (0, 0)
    return pl.pallas_call(
        _gdn_body,
        grid=(bsz,),
        in_specs=[
            pl.BlockSpec((1, seq, B_W), col(3)), pl.BlockSpec((1, seq, B_W), col(4)),
            pl.BlockSpec((1, seq, B_W), col(5)), pl.BlockSpec((1, seq, B_W), col(6)),
            pl.BlockSpec((1, seq, LANES), lambda b: (b, 0, 0)),
            pl.BlockSpec((CONV_WIDTH, 3 * B_W), vec),
            pl.BlockSpec((1, LANES), vec), pl.BlockSpec((1, LANES), vec), pl.BlockSpec((1, LANES), vec),
        ],
        out_specs=pl.BlockSpec((1, seq, B_W), lambda b: (b, 0, 0)),
        out_shape=jax.ShapeDtypeStruct((bsz, seq, B_W), BF16),
        scratch_shapes=[pltpu.VMEM((B_HEADS, 128, 128), F32)],
        compiler_params=pltpu.CompilerParams(
            dimension_semantics=("arbitrary",), vmem_limit_bytes=VMEM_LIMIT),
        name="gdn",
    )(p3, p3, p3, p3, sm3, conv_w, alog_v, dtb_v, gnorm)


def _ret_body(qk_ref, v_ref, g_ref, cos_ref, sin_ref, dmat_ref, xi_ref, zeta_ref, gch_ref,
              y_ref, s_scr):
    c = RET_BLK
    nc = v_ref.shape[1] // c
    lane = lax.broadcasted_iota(I32, (c, LANES), 1)
    first = (lane & 32) == 0
    lo_head = lane < 64
    s_scr[...] = jnp.zeros_like(s_scr)

    def chunk(n, carry):
        r0 = pl.multiple_of(n * c, c)
        qk = qk_ref[0, pl.ds(r0, c), :].astype(F32)
        cs = cos_ref[pl.ds(r0, c), :]
        sn = sin_ref[pl.ds(r0, c), :]
        for p in range(C_HEADS // 2):
            qp = _rot_half32(qk[:, p * 128:(p + 1) * 128], cs, sn, first)
            kp = _rot_half32(qk[:, C_QK + p * 128:C_QK + (p + 1) * 128], cs, sn, first) * (C_KEY_DIM ** -0.5)
            qm = [jnp.where(lo_head, qp, 0.0).astype(BF16), jnp.where(lo_head, 0.0, qp).astype(BF16)]
            km = [jnp.where(lo_head, kp, 0.0), jnp.where(lo_head, 0.0, kp)]
            sc = _nt_dot(jnp.concatenate(qm, axis=0), kp.astype(BF16))
            for e in range(2):
                h = 2 * p + e
                sl = slice(h * 128, (h + 1) * 128)
                vh = v_ref[0, pl.ds(r0, c), sl].astype(F32)
                scores = sc[e * c:(e + 1) * c] * dmat_ref[h]
                st = s_scr[h]
                o = _dot(scores.astype(BF16), vh.astype(BF16)) + _dot(qm[e], st.astype(BF16)) * xi_ref[h]
                s_scr[h] = st * gch_ref[h, 0:1, :] + _dot(km[e].T.astype(BF16), (vh * zeta_ref[h]).astype(BF16))
                on = o * lax.rsqrt(jnp.mean(o * o, axis=-1, keepdims=True) + EPS)
                go = g_ref[0, pl.ds(r0, c), sl].astype(F32)
                y_ref[0, pl.ds(r0, c), sl] = (on * _silu(go)).astype(y_ref.dtype)
        return carry

    lax.fori_loop(0, nc, chunk, 0)


def _ret_call(p3, cos_t, sin_t, dmat, xi, zeta, gch):
    bsz, seq, _ = p3.shape
    c = RET_BLK
    z2 = lambda b: (0, 0)
    z3 = lambda b: (0, 0, 0)
    return pl.pallas_call(
        _ret_body,
        grid=(bsz,),
        in_specs=[
            pl.BlockSpec((1, seq, 512), lambda b: (b, 0, 7)),
            pl.BlockSpec((1, seq, 512), lambda b: (b, 0, 8)),
            pl.BlockSpec((1, seq, 512), lambda b: (b, 0, 9)),
            pl.BlockSpec((seq, LANES), z2), pl.BlockSpec((seq, LANES), z2),
            pl.BlockSpec((C_HEADS, c, c), z3),
            pl.BlockSpec((C_HEADS, c, LANES), z3), pl.BlockSpec((C_HEADS, c, LANES), z3),
            pl.BlockSpec((C_HEADS, 8, LANES), z3),
        ],
        out_specs=pl.BlockSpec((1, seq, C_W), lambda b: (b, 0, 0)),
        out_shape=jax.ShapeDtypeStruct((bsz, seq, C_W), BF16),
        scratch_shapes=[pltpu.VMEM((C_HEADS, 128, 128), F32)],
        compiler_params=pltpu.CompilerParams(
            dimension_semantics=("arbitrary",), vmem_limit_bytes=VMEM_LIMIT),
        name="ret",
    )(p3, p3, p3, cos_t, sin_t, dmat, xi, zeta, gch)


def _merge_body(ya_ref, yb_ref, yc_ref, g0_ref, g1_ref, g2_ref, x_ref, wb_ref, wo_ref, ln_ref, o_ref):
    def gate(ref):
        return _sigmoid(ref[...].astype(F32))
    merged = (gate(g0_ref) * _dot(ya_ref[...], wb_ref[0:A_W, :])
              + gate(g1_ref) * _dot(yb_ref[...], wb_ref[A_W:A_W + B_W, :])
              + gate(g2_ref) * _dot(yc_ref[...], wb_ref[A_W + B_W:, :]))
    o = _dot(merged.astype(BF16), wo_ref[...])
    on = o * lax.rsqrt(jnp.mean(o * o, axis=-1, keepdims=True) + EPS) * ln_ref[...]
    o_ref[...] = x_ref[...] + on


def _merge_call(ya, yb, yc, p2d, x2d, wb, wo, ln):
    n = x2d.shape[0]
    tm = min(512, n)
    rowb = lambda i: (i, 0)
    cst = lambda i: (0, 0)
    gcol = lambda g: (lambda i: (i, g))
    return pl.pallas_call(
        _merge_body,
        grid=(n // tm,),
        in_specs=[
            pl.BlockSpec((tm, A_W), rowb), pl.BlockSpec((tm, B_W), rowb), pl.BlockSpec((tm, C_W), rowb),
            pl.BlockSpec((tm, D_MODEL), gcol(5)), pl.BlockSpec((tm, D_MODEL), gcol(6)),
            pl.BlockSpec((tm, D_MODEL), gcol(7)),
            pl.BlockSpec((tm, D_MODEL), rowb),
            pl.BlockSpec((A_W + B_W + C_W, D_MODEL), cst),
            pl.BlockSpec((D_MODEL, D_MODEL), cst),
            pl.BlockSpec((1, D_MODEL), cst),
        ],
        out_specs=pl.BlockSpec((tm, D_MODEL), rowb),
        out_shape=jax.ShapeDtypeStruct((n, D_MODEL), F32),
        compiler_params=pltpu.CompilerParams(
            dimension_semantics=("arbitrary",), vmem_limit_bytes=VMEM_LIMIT),
        name="merge",
    )(ya, yb, yc, p2d, p2d, p2d, x2d, wb, wo, ln)


def _mlp_body(x_ref, lnpre_ref, wu_ref, wd_ref, lnpost_ref, o_ref):
    x = x_ref[...]
    h = (x * lax.rsqrt(jnp.mean(x * x, axis=-1, keepdims=True) + EPS) * lnpre_ref[...]).astype(BF16)
    ff = jnp.zeros(x.shape, F32)
    for cidx in range(D_FF // D_MODEL):
        sl = slice(cidx * D_MODEL, (cidx + 1) * D_MODEL)
        u = jnp.maximum(_dot(h, wu_ref[:, sl]), 0.0)
        ff = ff + _dot((u * u).astype(BF16), wd_ref[sl, :])
    o_ref[...] = x + ff * lax.rsqrt(jnp.mean(ff * ff, axis=-1, keepdims=True) + EPS) * lnpost_ref[...]


def _mlp_call(x2d, lnpre, wu, wd, lnpost):
    n = x2d.shape[0]
    tm = min(512, n)
    rowb = lambda i: (i, 0)
    cst = lambda i: (0, 0)
    return pl.pallas_call(
        _mlp_body,
        grid=(n // tm,),
        in_specs=[
            pl.BlockSpec((tm, D_MODEL), rowb),
            pl.BlockSpec((1, D_MODEL), cst),
            pl.BlockSpec((D_MODEL, D_FF), cst),
            pl.BlockSpec((D_FF, D_MODEL), cst),
            pl.BlockSpec((1, D_MODEL), cst),
        ],
        out_specs=pl.BlockSpec((tm, D_MODEL), rowb),
        out_shape=jax.ShapeDtypeStruct((n, D_MODEL), F32),
        compiler_params=pltpu.CompilerParams(
            dimension_semantics=("arbitrary",), vmem_limit_bytes=VMEM_LIMIT),
        name="mlp",
    )(x2d, lnpre, wu, wd, lnpost)


def _rope_tables(seq):
    pos = jnp.arange(seq, dtype=F32)
    inv_a = ROPE_THETA ** (-jnp.arange(0, A_HEAD_DIM, 2, dtype=F32) / A_HEAD_DIM)
    ang_a = pos[:, None] * inv_a[None, :]
    cosa = jnp.concatenate([jnp.cos(ang_a)] * 2, axis=1)
    sina = jnp.concatenate([-jnp.sin(ang_a), jnp.sin(ang_a)], axis=1)
    inv_i = ROPE_THETA ** (-jnp.arange(0, IDX_DIM, 2, dtype=F32) / IDX_DIM)
    ang_i = pos[:, None] * inv_i[None, :]
    cosi = jnp.concatenate([jnp.cos(ang_i)] * 4, axis=1)
    sini = jnp.concatenate([-jnp.sin(ang_i), jnp.sin(ang_i)] * 2, axis=1)
    inv_c = 1.0 / (ROPE_THETA ** jnp.linspace(0.0, 1.0, C_KEY_DIM // 2, dtype=F32))
    ang_c = pos[:, None] * inv_c[None, :]
    cosc = jnp.concatenate([jnp.cos(ang_c)] * 4, axis=1)
    sinc = jnp.concatenate([-jnp.sin(ang_c), jnp.sin(ang_c)] * 2, axis=1)
    return (cosa, sina, cosi, sini), (cosc, sinc)


def _ret_tables():
    c = RET_BLK
    log_gamma = jnp.log(1.0 - 2.0 ** (-5.0 - jnp.arange(C_HEADS, dtype=F32)))
    r = jnp.arange(c, dtype=F32)
    rel = r[:, None] - r[None, :]
    dmat = jnp.where(rel >= 0, jnp.exp(jnp.maximum(rel, 0.0)[None] * log_gamma[:, None, None]), 0.0)
    xi = jnp.exp((r + 1.0)[None] * log_gamma[:, None])
    zeta = jnp.exp((c - 1.0 - r)[None] * log_gamma[:, None])
    gch = jnp.exp(c * log_gamma)
    xi_b = jnp.broadcast_to(xi[:, :, None], (C_HEADS, c, LANES))
    zeta_b = jnp.broadcast_to(zeta[:, :, None], (C_HEADS, c, LANES))
    gch_b = jnp.broadcast_to(gch[:, None, None], (C_HEADS, 8, LANES))
    return dmat, xi_b, zeta_b, gch_b


def _lane_vec(vals, ofs):
    v = jnp.zeros((1, LANES), F32)
    return v.at[0, ofs:ofs + vals.shape[0]].set(vals.astype(F32))


def kernel(x, ln_mix_pre, w_in, gdn_conv, gdn_a_log, gdn_dt_bias, gdn_norm, w_branch, w_out,
           ln_mix_post, ln_mlp_pre, w_up, w_down, ln_mlp_post):
    bsz, seq, _ = x.shape
    depth = w_in.shape[0]
    n = bsz * seq
    n_sel = min(TOPK_MAX, seq // 4)
    assert seq % BLK == 0 and n_sel % BLK == 0
    assert seq <= 16 * 256
    assert seq % (2 * KBLK) == 0
    dsa_tabs, (cosc, sinc) = _rope_tables(seq)
    dmat, xi_b, zeta_b, gch_b = _ret_tables()

    x2d = x.reshape(n, D_MODEL)
    for l in range(depth):
        w_big, w_small = _prep_w_in(w_in[l])
        p2d, sm2d = _proj_call(x2d, ln_mix_pre[l][None, :], w_big, w_small)
        p3 = p2d.reshape(bsz, seq, P_WIDTH)
        sm3 = sm2d.reshape(bsz, seq, LANES)
        y_a = _dsa_call(p3, sm3, dsa_tabs, n_sel)
        y_b = _gdn_call(p3, sm3, gdn_conv[l], _lane_vec(gdn_a_log[l], SM_BA),
                        _lane_vec(gdn_dt_bias[l], SM_BA), gdn_norm[l][None, :])
        y_c = _ret_call(p3, cosc, sinc, dmat, xi_b, zeta_b, gch_b)
        x2d = _merge_call(y_a.reshape(n, A_W), y_b.reshape(n, B_W), y_c.reshape(n, C_W), p2d, x2d,
                          w_branch[l].astype(BF16), w_out[l].astype(BF16), ln_mix_post[l][None, :])
        x2d = _mlp_call(x2d, ln_mlp_pre[l][None, :], w_up[l].astype(BF16), w_down[l].astype(BF16),
                        ln_mlp_post[l][None, :])
    return x2d.reshape(bsz, seq, D_MODEL)
```

```python
import functools
import math

import numpy as np
import jax
import jax.numpy as jnp
from jax import lax
from jax.experimental import pallas as pl
from jax.experimental.pallas import tpu as pltpu

F32 = jnp.float32
BF16 = jnp.bfloat16
I32 = jnp.int32

D_MODEL = 1024
A_HEADS = 4
A_HEAD_DIM = 128
IDX_HEADS = 8
IDX_DIM = 64
TOPK_MAX = 256
B_HEADS = 4
B_HEAD_DIM = 128
CONV_WIDTH = 4
C_HEADS = 4
C_KEY_DIM = 64
C_VAL_DIM = 128
D_FF = 4 * D_MODEL
ROPE_THETA = 10000.0
EPS = 1e-6
N_BRANCH = 3
A_W = A_HEADS * A_HEAD_DIM
B_W = B_HEADS * B_HEAD_DIM
C_W = C_HEADS * C_VAL_DIM
C_QK = C_HEADS * C_KEY_DIM

LANES = 128
BLK = 128
KBLK = 256
GDN_BLK = 128
GDN_CHUNK = 64
GDN_BASE = 8
RET_BLK = 128
VMEM_LIMIT = 52 * 1024 * 1024

_OFF = {}
_o = 0
for _name, _w in (("a_q", A_W), ("a_k", A_HEAD_DIM), ("a_v", A_HEAD_DIM),
                  ("i_q", IDX_HEADS * IDX_DIM), ("i_k", IDX_DIM), ("i_w", IDX_HEADS),
                  ("b_q", B_W), ("b_k", B_W), ("b_v", B_W), ("b_a", B_HEADS), ("b_b", B_HEADS),
                  ("b_z", B_W), ("c_q", C_QK), ("c_k", C_QK), ("c_v", C_W), ("c_g", C_W),
                  ("gate", N_BRANCH * D_MODEL)):
    _OFF[_name] = (_o, _w)
    _o += _w

P_WIDTH = 8192
P_TILE = 1024
SM_IW = 0
SM_BA = 8
SM_BB = 12

KEY_NEG_INF = -2139095041
INT_MIN = -2147483648
HI16 = -65536
MIN_NORMAL_BITS = 0x00800000


def _cols(w, name):
    o, n = _OFF[name]
    return w[:, o:o + n]


def _prep_w_in(w):
    ik = _cols(w, "i_k")
    big = jnp.concatenate([
        _cols(w, "a_q"), _cols(w, "a_k"), _cols(w, "a_v"), ik, ik, ik, ik,
        _cols(w, "i_q"),
        _cols(w, "b_q"), _cols(w, "b_k"), _cols(w, "b_v"), _cols(w, "b_z"),
        _cols(w, "c_q"), _cols(w, "c_k"), _cols(w, "c_v"), _cols(w, "c_g"),
        _cols(w, "gate")], axis=1)
    assert big.shape[1] == P_WIDTH
    small = jnp.concatenate([
        _cols(w, "i_w"), _cols(w, "b_a"), _cols(w, "b_b"),
        jnp.zeros((w.shape[0], LANES - 16), w.dtype)], axis=1)
    return big.astype(BF16), small.astype(BF16)


def _nt_dot(a, b):
    return lax.dot_general(a, b, (((1,), (1,)), ((), ())), preferred_element_type=F32)


def _dot(a, b):
    return jnp.dot(a, b, preferred_element_type=F32)


def _sigmoid(x):
    return 0.5 + 0.5 * jnp.tanh(0.5 * x)


def _silu(x):
    return x * _sigmoid(x)


def _softplus(x):
    return jnp.maximum(x, 0.0) + jnp.log(1.0 + jnp.exp(-jnp.abs(x)))


def _rot_half64(x, cos, sin_signed):
    return x * cos + pltpu.roll(x, 64, 1) * sin_signed


def _rot_half32(x, cos, sin_signed, first):
    partner = jnp.where(first, pltpu.roll(x, 96, 1), pltpu.roll(x, 32, 1))
    return x * cos + partner * sin_signed


def _proj_body(x_ref, gain_ref, w_ref, ws_ref, p_ref, s_ref):
    x = x_ref[...]
    ms = jnp.mean(x * x, axis=-1, keepdims=True)
    h = (x * lax.rsqrt(ms + EPS) * gain_ref[...]).astype(BF16)
    s_ref[...] = _dot(h, ws_ref[...])
    for t in range(P_WIDTH // P_TILE):
        sl = slice(t * P_TILE, (t + 1) * P_TILE)
        p_ref[:, sl] = _dot(h, w_ref[:, sl]).astype(p_ref.dtype)


def _proj_call(x2d, gain, w_big, w_small):
    n = x2d.shape[0]
    tm = min(512, n)
    once = pl.Buffered(1)
    return pl.pallas_call(
        _proj_body,
        grid=(n // tm,),
        in_specs=[
            pl.BlockSpec((tm, D_MODEL), lambda i: (i, 0)),
            pl.BlockSpec((1, D_MODEL), lambda i: (0, 0)),
            pl.BlockSpec((D_MODEL, P_WIDTH), lambda i: (0, 0), pipeline_mode=once),
            pl.BlockSpec((D_MODEL, LANES), lambda i: (0, 0), pipeline_mode=once),
        ],
        out_specs=[
            pl.BlockSpec((tm, P_WIDTH), lambda i: (i, 0)),
            pl.BlockSpec((tm, LANES), lambda i: (i, 0)),
        ],
        out_shape=[
            jax.ShapeDtypeStruct((n, P_WIDTH), BF16),
            jax.ShapeDtypeStruct((n, LANES), F32),
        ],
        compiler_params=pltpu.CompilerParams(
            dimension_semantics=("arbitrary",), vmem_limit_bytes=VMEM_LIMIT),
        name="proj",
    )(x2d, gain, w_big, w_small)


def _dsa_body(n_sel, aq_ref, kv_ref, iq_ref, sm_ref,
              cosa_ref, sina_ref, cosi_ref, sini_ref,
              cosaq_ref, sinaq_ref, cosiq_ref, siniq_ref, tri_ref,
              y_ref, ka_scr, ki_scr, vt_scr, key_scr, khi_scr, rank_scr, s_scr, acc_scr):
    j = pl.program_id(1)
    nkb = ka_scr.shape[0]
    kblk = ka_scr.shape[1]
    seq = nkb * kblk
    nk = (j * BLK + BLK + kblk - 1) // kblk
    lane = lax.broadcasted_iota(I32, (BLK, LANES), 1)
    klane = lax.broadcasted_iota(I32, (kblk, LANES), 1)
    krow = lax.broadcasted_iota(I32, (kblk, LANES), 0)
    first = (lane & 32) == 0
    kfirst = (klane & 32) == 0
    lo_head = lane < 64

    @pl.when(j == 0)
    def _():
        def prep(kb, c):
            r0 = pl.multiple_of(kb * kblk, kblk)
            kv = kv_ref[0, pl.ds(r0, kblk), :].astype(F32)
            ca = cosa_ref[pl.ds(r0, kblk), :]
            sa = sina_ref[pl.ds(r0, kblk), :]
            ci = cosi_ref[pl.ds(r0, kblk), :]
            si = sini_ref[pl.ds(r0, kblk), :]
            ka_scr[kb] = _rot_half64(kv[:, 0:128], ca, sa).astype(BF16)
            ki_scr[kb] = _rot_half32(kv[:, 256:384], ci, si, kfirst).astype(BF16)
            vt_scr[kb] = kv[:, 128:256].T.astype(BF16)
            return c
        lax.fori_loop(0, nkb, prep, 0)

    aq = aq_ref[0].astype(F32)
    caq = cosaq_ref[...]
    saq = sinaq_ref[...]
    qa = jnp.concatenate(
        [(_rot_half64(aq[:, h * 128:(h + 1) * 128], caq, saq) * (A_HEAD_DIM ** -0.5)).astype(BF16)
         for h in range(A_HEADS)], axis=0)

    iq = iq_ref[0].astype(F32)
    ciq = cosiq_ref[...]
    siq = siniq_ref[...]
    rows = []
    for p in range(IDX_HEADS // 2):
        rp = _rot_half32(iq[:, p * 128:(p + 1) * 128], ciq, siq, first)
        rows.append(jnp.where(lo_head, rp, 0.0).astype(BF16))
        rows.append(jnp.where(lo_head, 0.0, rp).astype(BF16))
    qm = jnp.concatenate(rows, axis=0)

    w_t = sm_ref[0].T * ((IDX_DIM ** -0.5) * (IDX_HEADS ** -0.5))

    def score_blk(kb, c):
        lg = _nt_dot(ki_scr[kb], qm)
        s_scr[kb] = _nt_dot(ka_scr[kb], qa)
        sc = jnp.zeros((kblk, LANES), F32)
        for h in range(IDX_HEADS):
            sc = sc + jnp.maximum(lg[:, h * BLK:(h + 1) * BLK], 0.0) * w_t[SM_IW + h:SM_IW + h + 1, :]
        sc = jnp.where(sc == 0.0, 0.0, sc)
        bits = pltpu.bitcast(sc, I32)
        key = bits ^ ((bits >> 31) & 0x7FFFFFFF)
        causal = (kb * kblk + krow) <= (j * BLK + klane)
        key_scr[kb] = jnp.where(causal, key, KEY_NEG_INF)
        hi = pltpu.bitcast(bits & HI16, F32)
        khi_scr[kb] = jnp.where(causal, hi, -jnp.inf).astype(BF16)
        return c
    lax.fori_loop(0, nk, score_blk, 0)

    @pl.when(nk % 2 == 1)
    def _():
        key_scr[nk] = jnp.full((kblk, LANES), KEY_NEG_INF, I32)
        khi_scr[nk] = jnp.full((kblk, LANES), -jnp.inf, BF16)
    nk2 = (nk + 1) // 2

    def count_hi(cf):
        one = jnp.ones((), BF16)
        zero = jnp.zeros((), BF16)

        def body(kp, acc):
            parts = []
            for kb in (2 * kp, 2 * kp + 1):
                m = jnp.where(khi_scr[kb] >= cf, one, zero)
                parts += [m[r * 16:(r + 1) * 16, :] for r in range(kblk // 16)]
            while len(parts) > 1:
                parts = [a + b for a, b in zip(parts[0::2], parts[1::2])]
            return acc + parts[0]
        acc = lax.fori_loop(0, nk2, body, jnp.zeros((16, LANES), BF16))
        return acc.astype(F32).sum(axis=0, keepdims=True)

    def count(pred):
        def body(kp, acc):
            for kb in (2 * kp, 2 * kp + 1):
                m = pred(key_scr[kb], kb * kblk + krow).astype(I32)
                acc = acc + m.reshape(kblk // 8, 8, LANES).sum(axis=0)
            return acc
        acc = lax.fori_loop(0, nk2, body, jnp.zeros((8, LANES), I32))
        return acc.sum(axis=0, keepdims=True)

    def search():
        def hi_step(i, tu):
            cand = tu | lax.shift_left(jnp.int32(1), 31 - i)
            cs = cand ^ INT_MIN
            fbits = (cs ^ ((cs >> 31) & 0x7FFFFFFF)) & HI16
            fbits = jnp.where((fbits > 0) & (fbits < MIN_NORMAL_BITS), MIN_NORMAL_BITS, fbits)
            cnt = count_hi(pltpu.bitcast(fbits, F32).astype(BF16))
            return jnp.where(cnt >= n_sel, cand, tu)

        def bit_step(i, tu):
            cand = tu | lax.shift_left(jnp.int32(1), 31 - i)
            cs = cand ^ INT_MIN
            cnt = count(lambda k, idx: k >= cs)
            return jnp.where(cnt >= n_sel, cand, tu)
        tu = lax.fori_loop(0, 16, hi_step, jnp.zeros((1, LANES), I32))
        tu = lax.fori_loop(16, 32, bit_step, tu)
        return tu ^ INT_MIN

    def take_all():
        return jnp.full((1, LANES), KEY_NEG_INF, I32)

    t = lax.cond((j + 1) * BLK > n_sel, search, take_all)

    def rank_blk(kb, carry):
        off, cgt = carry
        key = key_scr[kb]
        tie = jnp.where(key == t, 1.0, 0.0).astype(BF16)
        pre = _dot(tri_ref[...], tie)
        rank_scr[kb] = pre + off
        cgt = cgt + (key > t).astype(I32).reshape(kblk // 8, 8, LANES).sum(axis=0)
        return off + pre[kblk - 1:kblk, :], cgt
    _, cgt = lax.fori_loop(0, nk, rank_blk, (jnp.zeros((1, LANES), F32), jnp.zeros((8, LANES), I32)))
    need = jnp.where(t == KEY_NEG_INF, 0, n_sel - cgt.sum(axis=0, keepdims=True)).astype(F32)

    def mask_blk(kb, macc):
        key = key_scr[kb]
        sel = (key > t) | ((key == t) & (rank_scr[kb] <= need))
        s = s_scr[kb]
        sm = jnp.concatenate(
            [jnp.where(sel, s[:, h * BLK:(h + 1) * BLK], -jnp.inf) for h in range(A_HEADS)], axis=1)
        s_scr[kb] = sm
        return jnp.maximum(macc, sm.reshape(kblk // 8, 8, A_HEADS * BLK).max(axis=0))
    macc = lax.fori_loop(0, nk, mask_blk, jnp.full((8, A_HEADS * BLK), -jnp.inf, F32))
    m = macc.max(axis=0, keepdims=True)

    acc_scr[...] = jnp.zeros_like(acc_scr)

    def pv_blk(kb, lacc):
        p = jnp.exp(s_scr[kb] - m)
        acc_scr[...] += _dot(vt_scr[kb], p.astype(BF16))
        return lacc + p.reshape(kblk // 8, 8, A_HEADS * BLK).sum(axis=0)
    lacc = lax.fori_loop(0, nk, pv_blk, jnp.zeros((8, A_HEADS * BLK), F32))
    inv_l = 1.0 / lacc.sum(axis=0, keepdims=True)
    o_t = acc_scr[...] * inv_l
    for h in range(A_HEADS):
        y_ref[0, :, h * 128:(h + 1) * 128] = o_t[:, h * BLK:(h + 1) * BLK].T.astype(y_ref.dtype)


def _dsa_call(p3, sm3, tabs, n_sel):
    bsz, seq, _ = p3.shape
    nqb = seq // BLK
    kblk = min(KBLK, seq)
    nkb = seq // kblk
    cosa, sina, cosi, sini = tabs
    full = lambda b, j: (0, 0)
    qblk = lambda b, j: (j, 0)
    return pl.pallas_call(
        functools.partial(_dsa_body, n_sel),
        grid=(bsz, nqb),
        in_specs=[
            pl.BlockSpec((1, BLK, 512), lambda b, j: (b, j, 0)),
            pl.BlockSpec((1, seq, 512), lambda b, j: (b, 0, 1)),
            pl.BlockSpec((1, BLK, 512), lambda b, j: (b, j, 2)),
            pl.BlockSpec((1, BLK, LANES), lambda b, j: (b, j, 0)),
            pl.BlockSpec((seq, LANES), full), pl.BlockSpec((seq, LANES), full),
            pl.BlockSpec((seq, LANES), full), pl.BlockSpec((seq, LANES), full),
            pl.BlockSpec((BLK, LANES), qblk), pl.BlockSpec((BLK, LANES), qblk),
            pl.BlockSpec((BLK, LANES), qblk), pl.BlockSpec((BLK, LANES), qblk),
            pl.BlockSpec((kblk, kblk), full),
        ],
        out_specs=pl.BlockSpec((1, BLK, A_W), lambda b, j: (b, j, 0)),
        out_shape=jax.ShapeDtypeStruct((bsz, seq, A_W), BF16),
        scratch_shapes=[
            pltpu.VMEM((nkb, kblk, 128), BF16),
            pltpu.VMEM((nkb, kblk, 128), BF16),
            pltpu.VMEM((nkb, 128, kblk), BF16),
            pltpu.VMEM((nkb, kblk, LANES), I32),
            pltpu.VMEM((nkb, kblk, LANES), BF16),
            pltpu.VMEM((nkb, kblk, LANES), F32),
            pltpu.VMEM((nkb, kblk, A_HEADS * BLK), F32),
            pltpu.VMEM((128, A_HEADS * BLK), F32),
        ],
        compiler_params=pltpu.CompilerParams(
            dimension_semantics=("arbitrary", "arbitrary"), vmem_limit_bytes=VMEM_LIMIT),
        name="dsa",
    )(p3, p3, p3, sm3, cosa, sina, cosi, sini, cosa, sina, cosi, sini, jnp.tril(jnp.ones((kblk, kblk), BF16)))


def _gdn_body(q_ref, k_ref, v_ref, z_ref, sm_ref, cw_ref, alog_ref, dtb_ref, gn_ref,
              y_ref, s_scr):
    c = GDN_BLK
    hc = GDN_CHUNK
    nc = q_ref.shape[1] // c
    row = lax.broadcasted_iota(I32, (c, LANES), 0)
    col = lax.broadcasted_iota(I32, (c, LANES), 1)

    def same_block(size):
        sh = int(math.log2(size))
        return (row >> sh) == (col >> sh)

    same_chunk = same_block(hc)
    incl = same_chunk & (row >= col)
    strict = same_chunk & (row > col)
    eye = (row == col).astype(F32)
    sizes = [GDN_BASE * 2 ** i for i in range(int(math.log2(hc // GDN_BASE)) + 1)]
    base_mask = same_block(GDN_BASE) & (row > col)
    level_masks = [same_block(big) & jnp.logical_not(same_block(small))
                   for small, big in zip(sizes[:-1], sizes[1:])]
    top_rows = row < hc
    s_scr[...] = jnp.zeros_like(s_scr)
    row8 = lax.broadcasted_iota(I32, (8, B_W), 0)

    def split(a):
        hi = a.astype(BF16)
        return hi, (a - hi.astype(F32)).astype(BF16)

    def mm3(a, b):
        ah, al = a
        bh, bl = b
        r = _dot(ah, bh)
        if bl is not None:
            r = r + _dot(ah, bl)
        if al is not None:
            r = r + _dot(al, bh)
        return r

    def unit_lower_inverse(ms):
        n1 = [(jnp.where(base_mask, -m, 0.0).astype(BF16), None) for m in ms]
        d = [eye + x[0].astype(F32) for x in n1]
        pw = n1
        for _ in range(int(math.log2(GDN_BASE)) - 1):
            pw = [split(mm3(x, x)) for x in pw]
            d = [x + mm3(split(x), p) for x, p in zip(d, pw)]
        for lm in level_masks:
            ds = [split(x) for x in d]
            t = [split(mm3(y, (jnp.where(lm, m, 0.0).astype(BF16), None))) for y, m in zip(ds, ms)]
            d = [x - mm3(tt, y) for x, tt, y in zip(d, t, ds)]
        return d

    def conv_silu(ref, n, wofs):
        r0 = pl.multiple_of(n * c, c)
        cur = ref[0, pl.ds(r0, c), :].astype(F32)
        pr0 = pl.multiple_of(jnp.maximum(n, 1) * c - 16, 16)
        tail = ref[0, pl.ds(pr0, 16), :].astype(F32) * jnp.where(n > 0, 1.0, 0.0)
        y = cur * cw_ref[CONV_WIDTH - 1:CONV_WIDTH, wofs:wofs + B_W]
        for s in range(1, CONV_WIDTH):
            sh = pltpu.roll(cur, s, 0)
            top = jnp.where(row8 < s, pltpu.roll(tail, s, 0)[:8], sh[:8])
            sh = jnp.concatenate([top, sh[8:]], axis=0)
            y = y + sh * cw_ref[CONV_WIDTH - 1 - s:CONV_WIDTH - s, wofs:wofs + B_W]
        return _silu(y)

    def chunk(n, carry):
        r0 = pl.multiple_of(n * c, c)
        qc = conv_silu(q_ref, n, 0)
        kc = conv_silu(k_ref, n, B_W)
        vc = conv_silu(v_ref, n, 2 * B_W)
        sm = sm_ref[0, pl.ds(r0, c), :]
        g = -jnp.exp(alog_ref[...]) * _softplus(sm + dtb_ref[...])
        beta = _sigmoid(sm)
        s = 1
        while s < hc:
            g = g + jnp.where((row & (hc - 1)) >= s, pltpu.roll(g, s, 0), 0.0)
            s *= 2
        g_t = g.T
        heads = range(B_HEADS)
        sls = [slice(h * 128, (h + 1) * 128) for h in heads]
        gcol = [jnp.broadcast_to(g[:, SM_BA + h:SM_BA + h + 1], (c, LANES)) for h in heads]
        grow = [jnp.broadcast_to(g_t[SM_BA + h:SM_BA + h + 1, :], (c, LANES)) for h in heads]
        bcol = [jnp.broadcast_to(beta[:, SM_BB + h:SM_BB + h + 1], (c, LANES)) for h in heads]
        qh = [qc[:, sl] for sl in sls]
        kh = [kc[:, sl] for sl in sls]
        qh = [x * lax.rsqrt(jnp.sum(x * x, axis=-1, keepdims=True) + EPS) * (B_HEAD_DIM ** -0.5) for x in qh]
        kh = [x * lax.rsqrt(jnp.sum(x * x, axis=-1, keepdims=True) + EPS) for x in kh]
        decay = [jnp.exp(jnp.where(incl, gc - gr, -jnp.inf)) for gc, gr in zip(gcol, grow)]
        kbeta = [k * b for k, b in zip(kh, bcol)]
        prod = [_nt_dot(jnp.concatenate([kb.astype(BF16), q.astype(BF16)], axis=0), k.astype(BF16))
                for kb, q, k in zip(kbeta, qh, kh)]
        ms = [jnp.where(strict, p[:c] * dc, 0.0) for p, dc in zip(prod, decay)]
        a_intra = [(p[c:] * dc).astype(BF16) for p, dc in zip(prod, decay)]
        inv = unit_lower_inverse(ms)
        egc = [jnp.exp(x) for x in gcol]
        rhs = [jnp.concatenate([(vc[:, sl] * b).astype(BF16), (kb * e).astype(BF16)], axis=1)
               for sl, b, kb, e in zip(sls, bcol, kbeta, egc)]
        uw = [_dot(i.astype(BF16), r) for i, r in zip(inv, rhs)]
        u = [x[:, :128] for x in uw]
        wb = [x[:, 128:].astype(BF16) for x in uw]
        qg = [(q * e).astype(BF16) for q, e in zip(qh, egc)]
        kd_t = [(k * jnp.exp(jnp.where(top_rows, gc[hc - 1:hc, :], gc[c - 1:c, :]) - gc)).T
                for k, gc in zip(kh, gcol)]
        st = [s_scr[h] for h in heads]
        o_parts = [[] for _ in heads]
        for ci in range(c // hc):
            rs = slice(ci * hc, (ci + 1) * hc)
            in_chunk = (col >> int(math.log2(hc))) == ci
            stb = [x.astype(BF16) for x in st]
            v_new = [u[h][rs] - _dot(wb[h][rs], stb[h]) for h in heads]
            vn_pad = [jnp.concatenate([x] * (c // hc), axis=0).astype(BF16) for x in v_new]
            for h in heads:
                o_parts[h].append(_dot(qg[h][rs], stb[h]) + _dot(a_intra[h][rs], vn_pad[h]))
            st = [st[h] * jnp.exp(gcol[h][(ci + 1) * hc - 1:(ci + 1) * hc, :])
                  + _dot(jnp.where(in_chunk, kd_t[h], 0.0).astype(BF16), vn_pad[h]) for h in heads]
        for h in heads:
            s_scr[h] = st[h]
            o = jnp.concatenate(o_parts[h], axis=0)
            zo = z_ref[0, pl.ds(r0, c), sls[h]].astype(F32)
            on = o * lax.rsqrt(jnp.mean(o * o, axis=-1, keepdims=True) + EPS) * gn_ref[...]
            y_ref[0, pl.ds(r0, c), sls[h]] = (on * _silu(zo)).astype(y_ref.dtype)
        return carry

    lax.fori_loop(0, nc, chunk, 0)


def _gdn_call(p3, sm3, conv_w, alog_v, dtb_v, gnorm):
    bsz, seq, _ = p3.shape
    col = lambda g: (lambda b: (b, 0, g))
    vec = lambda b: (0, 0)
    return pl.pallas_call(
        _gdn_body,
        grid=(bsz,),
        in_specs=[
            pl.BlockSpec((1, seq, B_W), col(3)), pl.BlockSpec((1, seq, B_W), col(4)),
            pl.BlockSpec((1, seq, B_W), col(5)), pl.BlockSpec((1, seq, B_W), col(6)),
            pl.BlockSpec((1, seq, LANES), lambda b: (b, 0, 0)),
            pl.BlockSpec((CONV_WIDTH, 3 * B_W), vec),
            pl.BlockSpec((1, LANES), vec), pl.BlockSpec((1, LANES), vec), pl.BlockSpec((1, LANES), vec),
        ],
        out_specs=pl.BlockSpec((1, seq, B_W), lambda b: (b, 0, 0)),
        out_shape=jax.ShapeDtypeStruct((bsz, seq, B_W), BF16),
        scratch_shapes=[pltpu.VMEM((B_HEADS, 128, 128), F32)],
        compiler_params=pltpu.CompilerParams(
            dimension_semantics=("arbitrary",), vmem_limit_bytes=VMEM_LIMIT),
        name="gdn",
    )(p3, p3, p3, p3, sm3, conv_w, alog_v, dtb_v, gnorm)


def _ret_body(qk_ref, v_ref, g_ref, cos_ref, sin_ref, dmat_ref, xi_ref, zeta_ref, gch_ref,
              y_ref, s_scr):
    c = RET_BLK
    nc = v_ref.shape[1] // c
    lane = lax.broadcasted_iota(I32, (c, LANES), 1)
    first = (lane & 32) == 0
    lo_head = lane < 64
    s_scr[...] = jnp.zeros_like(s_scr)

    def chunk(n, carry):
        r0 = pl.multiple_of(n * c, c)
        qk = qk_ref[0, pl.ds(r0, c), :].astype(F32)
        cs = cos_ref[pl.ds(r0, c), :]
        sn = sin_ref[pl.ds(r0, c), :]
        for p in range(C_HEADS // 2):
            qp = _rot_half32(qk[:, p * 128:(p + 1) * 128], cs, sn, first)
            kp = _rot_half32(qk[:, C_QK + p * 128:C_QK + (p + 1) * 128], cs, sn, first) * (C_KEY_DIM ** -0.5)
            qm = [jnp.where(lo_head, qp, 0.0).astype(BF16), jnp.where(lo_head, 0.0, qp).astype(BF16)]
            km = [jnp.where(lo_head, kp, 0.0), jnp.where(lo_head, 0.0, kp)]
            sc = _nt_dot(jnp.concatenate(qm, axis=0), kp.astype(BF16))
            for e in range(2):
                h = 2 * p + e
                sl = slice(h * 128, (h + 1) * 128)
                vh = v_ref[0, pl.ds(r0, c), sl].astype(F32)
                scores = sc[e * c:(e + 1) * c] * dmat_ref[h]
                st = s_scr[h]
                o = _dot(scores.astype(BF16), vh.astype(BF16)) + _dot(qm[e], st.astype(BF16)) * xi_ref[h]
                s_scr[h] = st * gch_ref[h, 0:1, :] + _dot(km[e].T.astype(BF16), (vh * zeta_ref[h]).astype(BF16))
                on = o * lax.rsqrt(jnp.mean(o * o, axis=-1, keepdims=True) + EPS)
                go = g_ref[0, pl.ds(r0, c), sl].astype(F32)
                y_ref[0, pl.ds(r0, c), sl] = (on * _silu(go)).astype(y_ref.dtype)
        return carry

    lax.fori_loop(0, nc, chunk, 0)


def _ret_call(p3, cos_t, sin_t, dmat, xi, zeta, gch):
    bsz, seq, _ = p3.shape
    c = RET_BLK
    z2 = lambda b: (0, 0)
    z3 = lambda b: (0, 0, 0)
    return pl.pallas_call(
        _ret_body,
        grid=(bsz,),
        in_specs=[
            pl.BlockSpec((1, seq, 512), lambda b: (b, 0, 7)),
            pl.BlockSpec((1, seq, 512), lambda b: (b, 0, 8)),
            pl.BlockSpec((1, seq, 512), lambda b: (b, 0, 9)),
            pl.BlockSpec((seq, LANES), z2), pl.BlockSpec((seq, LANES), z2),
            pl.BlockSpec((C_HEADS, c, c), z3),
            pl.BlockSpec((C_HEADS, c, LANES), z3), pl.BlockSpec((C_HEADS, c, LANES), z3),
            pl.BlockSpec((C_HEADS, 8, LANES), z3),
        ],
        out_specs=pl.BlockSpec((1, seq, C_W), lambda b: (b, 0, 0)),
        out_shape=jax.ShapeDtypeStruct((bsz, seq, C_W), BF16),
        scratch_shapes=[pltpu.VMEM((C_HEADS, 128, 128), F32)],
        compiler_params=pltpu.CompilerParams(
            dimension_semantics=("arbitrary",), vmem_limit_bytes=VMEM_LIMIT),
        name="ret",
    )(p3, p3, p3, cos_t, sin_t, dmat, xi, zeta, gch)


def _merge_body(ya_ref, yb_ref, yc_ref, g0_ref, g1_ref, g2_ref, x_ref, wb_ref, wo_ref, ln_ref, o_ref):
    def gate(ref):
        return _sigmoid(ref[...].astype(F32))
    merged = (gate(g0_ref) * _dot(ya_ref[...], wb_ref[0:A_W, :])
              + gate(g1_ref) * _dot(yb_ref[...], wb_ref[A_W:A_W + B_W, :])
              + gate(g2_ref) * _dot(yc_ref[...], wb_ref[A_W + B_W:, :]))
    o = _dot(merged.astype(BF16), wo_ref[...])
    on = o * lax.rsqrt(jnp.mean(o * o, axis=-1, keepdims=True) + EPS) * ln_ref[...]
    o_ref[...] = x_ref[...] + on


def _merge_call(ya, yb, yc, p2d, x2d, wb, wo, ln):
    n = x2d.shape[0]
    tm = min(512, n)
    rowb = lambda i: (i, 0)
    cst = lambda i: (0, 0)
    gcol = lambda g: (lambda i: (i, g))
    return pl.pallas_call(
        _merge_body,
        grid=(n // tm,),
        in_specs=[
            pl.BlockSpec((tm, A_W), rowb), pl.BlockSpec((tm, B_W), rowb), pl.BlockSpec((tm, C_W), rowb),
            pl.BlockSpec((tm, D_MODEL), gcol(5)), pl.BlockSpec((tm, D_MODEL), gcol(6)),
            pl.BlockSpec((tm, D_MODEL), gcol(7)),
            pl.BlockSpec((tm, D_MODEL), rowb),
            pl.BlockSpec((A_W + B_W + C_W, D_MODEL), cst),
            pl.BlockSpec((D_MODEL, D_MODEL), cst),
            pl.BlockSpec((1, D_MODEL), cst),
        ],
        out_specs=pl.BlockSpec((tm, D_MODEL), rowb),
        out_shape=jax.ShapeDtypeStruct((n, D_MODEL), F32),
        compiler_params=pltpu.CompilerParams(
            dimension_semantics=("arbitrary",), vmem_limit_bytes=VMEM_LIMIT),
        name="merge",
    )(ya, yb, yc, p2d, p2d, p2d, x2d, wb, wo, ln)


def _mlp_body(x_ref, lnpre_ref, wu_ref, wd_ref, lnpost_ref, o_ref):
    x = x_ref[...]
    h = (x * lax.rsqrt(jnp.mean(x * x, axis=-1, keepdims=True) + EPS) * lnpre_ref[...]).astype(BF16)
    ff = jnp.zeros(x.shape, F32)
    for cidx in range(D_FF // D_MODEL):
        sl = slice(cidx * D_MODEL, (cidx + 1) * D_MODEL)
        u = jnp.maximum(_dot(h, wu_ref[:, sl]), 0.0)
        ff = ff + _dot((u * u).astype(BF16), wd_ref[sl, :])
    o_ref[...] = x + ff * lax.rsqrt(jnp.mean(ff * ff, axis=-1, keepdims=True) + EPS) * lnpost_ref[...]


def _mlp_call(x2d, lnpre, wu, wd, lnpost):
    n = x2d.shape[0]
    tm = min(512, n)
    rowb = lambda i: (i, 0)
    cst = lambda i: (0, 0)
    return pl.pallas_call(
        _mlp_body,
        grid=(n // tm,),
        in_specs=[
            pl.BlockSpec((tm, D_MODEL), rowb),
            pl.BlockSpec((1, D_MODEL), cst),
            pl.BlockSpec((D_MODEL, D_FF), cst),
            pl.BlockSpec((D_FF, D_MODEL), cst),
            pl.BlockSpec((1, D_MODEL), cst),
        ],
        out_specs=pl.BlockSpec((tm, D_MODEL), rowb),
        out_shape=jax.ShapeDtypeStruct((n, D_MODEL), F32),
        compiler_params=pltpu.CompilerParams(
            dimension_semantics=("arbitrary",), vmem_limit_bytes=VMEM_LIMIT),
        name="mlp",
    )(x2d, lnpre, wu, wd, lnpost)


def _rope_tables(seq):
    pos = jnp.arange(seq, dtype=F32)
    inv_a = ROPE_THETA ** (-jnp.arange(0, A_HEAD_DIM, 2, dtype=F32) / A_HEAD_DIM)
    ang_a = pos[:, None] * inv_a[None, :]
    cosa = jnp.concatenate([jnp.cos(ang_a)] * 2, axis=1)
    sina = jnp.concatenate([-jnp.sin(ang_a), jnp.sin(ang_a)], axis=1)
    inv_i = ROPE_THETA ** (-jnp.arange(0, IDX_DIM, 2, dtype=F32) / IDX_DIM)
    ang_i = pos[:, None] * inv_i[None, :]
    cosi = jnp.concatenate([jnp.cos(ang_i)] * 4, axis=1)
    sini = jnp.concatenate([-jnp.sin(ang_i), jnp.sin(ang_i)] * 2, axis=1)
    inv_c = 1.0 / (ROPE_THETA ** jnp.linspace(0.0, 1.0, C_KEY_DIM // 2, dtype=F32))
    ang_c = pos[:, None] * inv_c[None, :]
    cosc = jnp.concatenate([jnp.cos(ang_c)] * 4, axis=1)
    sinc = jnp.concatenate([-jnp.sin(ang_c), jnp.sin(ang_c)] * 2, axis=1)
    return (cosa, sina, cosi, sini), (cosc, sinc)


def _ret_tables():
    c = RET_BLK
    log_gamma = jnp.log(1.0 - 2.0 ** (-5.0 - jnp.arange(C_HEADS, dtype=F32)))
    r = jnp.arange(c, dtype=F32)
    rel = r[:, None] - r[None, :]
    dmat = jnp.where(rel >= 0, jnp.exp(jnp.maximum(rel, 0.0)[None] * log_gamma[:, None, None]), 0.0)
    xi = jnp.exp((r + 1.0)[None] * log_gamma[:, None])
    zeta = jnp.exp((c - 1.0 - r)[None] * log_gamma[:, None])
    gch = jnp.exp(c * log_gamma)
    xi_b = jnp.broadcast_to(xi[:, :, None], (C_HEADS, c, LANES))
    zeta_b = jnp.broadcast_to(zeta[:, :, None], (C_HEADS, c, LANES))
    gch_b = jnp.broadcast_to(gch[:, None, None], (C_HEADS, 8, LANES))
    return dmat, xi_b, zeta_b, gch_b


def _lane_vec(vals, ofs):
    v = jnp.zeros((1, LANES), F32)
    return v.at[0, ofs:ofs + vals.shape[0]].set(vals.astype(F32))


def kernel(x, ln_mix_pre, w_in, gdn_conv, gdn_a_log, gdn_dt_bias, gdn_norm, w_branch, w_out,
           ln_mix_post, ln_mlp_pre, w_up, w_down, ln_mlp_post):
    bsz, seq, _ = x.shape
    depth = w_in.shape[0]
    n = bsz * seq
    n_sel = min(TOPK_MAX, seq // 4)
    assert seq % BLK == 0 and n_sel % BLK == 0
    assert seq <= 16 * 256
    assert seq % (2 * KBLK) == 0
    dsa_tabs, (cosc, sinc) = _rope_tables(seq)
    dmat, xi_b, zeta_b, gch_b = _ret_tables()

    x2d = x.reshape(n, D_MODEL)
    for l in range(depth):
        w_big, w_small = _prep_w_in(w_in[l])
        p2d, sm2d = _proj_call(x2d, ln_mix_pre[l][None, :], w_big, w_small)
        p3 = p2d.reshape(bsz, seq, P_WIDTH)
        sm3 = sm2d.reshape(bsz, seq, LANES)
        y_a = _dsa_call(p3, sm3, dsa_tabs, n_sel)
        y_b = _gdn_call(p3, sm3, gdn_conv[l], _lane_vec(gdn_a_log[l], SM_BA),
                        _lane_vec(gdn_dt_bias[l], SM_BA), gdn_norm[l][None, :])
        y_c = _ret_call(p3, cosc, sinc, dmat, xi_b, zeta_b, gch_b)
        x2d = _merge_call(y_a.reshape(n, A_W), y_b.reshape(n, B_W), y_c.reshape(n, C_W), p2d, x2d,
                          w_branch[l].astype(BF16), w_out[l].astype(BF16), ln_mix_post[l][None, :])
        x2d = _mlp_call(x2d, ln_mlp_pre[l][None, :], w_up[l].astype(BF16), w_down[l].astype(BF16),
                        ln_mlp_post[l][None, :])
    return x2d.reshape(bsz, seq, D_MODEL)
```

```python
import functools
import math

import numpy as np
import jax
import jax.numpy as jnp
from jax import lax
from jax.experimental import pallas as pl
from jax.experimental.pallas import tpu as pltpu

F32 = jnp.float32
BF16 = jnp.bfloat16
I32 = jnp.int32

D_MODEL = 1024
A_HEADS = 4
A_HEAD_DIM = 128
IDX_HEADS = 8
IDX_DIM = 64
TOPK_MAX = 256
B_HEADS = 4
B_HEAD_DIM = 128
CONV_WIDTH = 4
C_HEADS = 4
C_KEY_DIM = 64
C_VAL_DIM = 128
D_FF = 4 * D_MODEL
ROPE_THETA = 10000.0
EPS = 1e-6
N_BRANCH = 3
A_W = A_HEADS * A_HEAD_DIM
B_W = B_HEADS * B_HEAD_DIM
C_W = C_HEADS * C_VAL_DIM
C_QK = C_HEADS * C_KEY_DIM

LANES = 128
BLK = 128
KBLK = 256
GDN_BLK = 128
GDN_CHUNK = 64
GDN_BASE = 8
GDN_UNROLL = 4
RET_BLK = 128
VMEM_LIMIT = 52 * 1024 * 1024

_OFF = {}
_o = 0
for _name, _w in (("a_q", A_W), ("a_k", A_HEAD_DIM), ("a_v", A_HEAD_DIM),
                  ("i_q", IDX_HEADS * IDX_DIM), ("i_k", IDX_DIM), ("i_w", IDX_HEADS),
                  ("b_q", B_W), ("b_k", B_W), ("b_v", B_W), ("b_a", B_HEADS), ("b_b", B_HEADS),
                  ("b_z", B_W), ("c_q", C_QK), ("c_k", C_QK), ("c_v", C_W), ("c_g", C_W),
                  ("gate", N_BRANCH * D_MODEL)):
    _OFF[_name] = (_o, _w)
    _o += _w

P_WIDTH = 8192
P_TILE = 1024
SM_IW = 0
SM_BA = 8
SM_BB = 12

KEY_NEG_INF = -2139095041
INT_MIN = -2147483648
HI16 = -65536
MIN_NORMAL_BITS = 0x00800000


def _cols(w, name):
    o, n = _OFF[name]
    return w[:, o:o + n]


def _prep_w_in(w):
    ik = _cols(w, "i_k")
    big = jnp.concatenate([
        _cols(w, "a_q"), _cols(w, "a_k"), _cols(w, "a_v"), ik, ik, ik, ik,
        _cols(w, "i_q"),
        _cols(w, "b_q"), _cols(w, "b_k"), _cols(w, "b_v"), _cols(w, "b_z"),
        _cols(w, "c_q"), _cols(w, "c_k"), _cols(w, "c_v"), _cols(w, "c_g"),
        _cols(w, "gate")], axis=1)
    assert big.shape[1] == P_WIDTH
    small = jnp.concatenate([
        _cols(w, "i_w"), _cols(w, "b_a"), _cols(w, "b_b"),
        jnp.zeros((w.shape[0], LANES - 16), w.dtype)], axis=1)
    return big.astype(BF16), small.astype(BF16)


def _nt_dot(a, b):
    return lax.dot_general(a, b, (((1,), (1,)), ((), ())), preferred_element_type=F32)


def _dot(a, b):
    return jnp.dot(a, b, preferred_element_type=F32)


def _sigmoid(x):
    return 0.5 + 0.5 * jnp.tanh(0.5 * x)


def _silu(x):
    return x * _sigmoid(x)


def _softplus(x):
    return jnp.maximum(x, 0.0) + jnp.log(1.0 + jnp.exp(-jnp.abs(x)))


def _rot_half64(x, cos, sin_signed):
    return x * cos + pltpu.roll(x, 64, 1) * sin_signed


def _rot_half32(x, cos, sin_signed, first):
    partner = jnp.where(first, pltpu.roll(x, 96, 1), pltpu.roll(x, 32, 1))
    return x * cos + partner * sin_signed


def _proj_body(x_ref, gain_ref, w_ref, ws_ref, p_ref, s_ref):
    x = x_ref[...]
    ms = jnp.mean(x * x, axis=-1, keepdims=True)
    h = (x * lax.rsqrt(ms + EPS) * gain_ref[...]).astype(BF16)
    s_ref[...] = _dot(h, ws_ref[...])
    for t in range(P_WIDTH // P_TILE):
        sl = slice(t * P_TILE, (t + 1) * P_TILE)
        p_ref[:, sl] = _dot(h, w_ref[:, sl]).astype(p_ref.dtype)


def _proj_call(x2d, gain, w_big, w_small):
    n = x2d.shape[0]
    tm = min(512, n)
    once = pl.Buffered(1)
    return pl.pallas_call(
        _proj_body,
        grid=(n // tm,),
        in_specs=[
            pl.BlockSpec((tm, D_MODEL), lambda i: (i, 0)),
            pl.BlockSpec((1, D_MODEL), lambda i: (0, 0)),
            pl.BlockSpec((D_MODEL, P_WIDTH), lambda i: (0, 0), pipeline_mode=once),
            pl.BlockSpec((D_MODEL, LANES), lambda i: (0, 0), pipeline_mode=once),
        ],
        out_specs=[
            pl.BlockSpec((tm, P_WIDTH), lambda i: (i, 0)),
            pl.BlockSpec((tm, LANES), lambda i: (i, 0)),
        ],
        out_shape=[
            jax.ShapeDtypeStruct((n, P_WIDTH), BF16),
            jax.ShapeDtypeStruct((n, LANES), F32),
        ],
        compiler_params=pltpu.CompilerParams(
            dimension_semantics=("arbitrary",), vmem_limit_bytes=VMEM_LIMIT),
        name="proj",
    )(x2d, gain, w_big, w_small)


def _dsa_body(n_sel, aq_ref, kv_ref, iq_ref, sm_ref,
              cosa_ref, sina_ref, cosi_ref, sini_ref,
              cosaq_ref, sinaq_ref, cosiq_ref, siniq_ref, tri_ref,
              y_ref, ka_scr, ki_scr, vt_scr, key_scr, khi_scr, rank_scr, s_scr, acc_scr):
    j = pl.program_id(1)
    nkb = ka_scr.shape[0]
    kblk = ka_scr.shape[1]
    seq = nkb * kblk
    nk = (j * BLK + BLK + kblk - 1) // kblk
    lane = lax.broadcasted_iota(I32, (BLK, LANES), 1)
    klane = lax.broadcasted_iota(I32, (kblk, LANES), 1)
    krow = lax.broadcasted_iota(I32, (kblk, LANES), 0)
    first = (lane & 32) == 0
    kfirst = (klane & 32) == 0
    lo_head = lane < 64

    @pl.when(j == 0)
    def _():
        def prep(kb, c):
            r0 = pl.multiple_of(kb * kblk, kblk)
            kv = kv_ref[0, pl.ds(r0, kblk), :].astype(F32)
            ca = cosa_ref[pl.ds(r0, kblk), :]
            sa = sina_ref[pl.ds(r0, kblk), :]
            ci = cosi_ref[pl.ds(r0, kblk), :]
            si = sini_ref[pl.ds(r0, kblk), :]
            ka_scr[kb] = _rot_half64(kv[:, 0:128], ca, sa).astype(BF16)
            ki_scr[kb] = _rot_half32(kv[:, 256:384], ci, si, kfirst).astype(BF16)
            vt_scr[kb] = kv[:, 128:256].T.astype(BF16)
            return c
        lax.fori_loop(0, nkb, prep, 0)

    aq = aq_ref[0].astype(F32)
    caq = cosaq_ref[...]
    saq = sinaq_ref[...]
    qa = jnp.concatenate(
        [(_rot_half64(aq[:, h * 128:(h + 1) * 128], caq, saq) * (A_HEAD_DIM ** -0.5)).astype(BF16)
         for h in range(A_HEADS)], axis=0)

    iq = iq_ref[0].astype(F32)
    ciq = cosiq_ref[...]
    siq = siniq_ref[...]
    rows = []
    for p in range(IDX_HEADS // 2):
        rp = _rot_half32(iq[:, p * 128:(p + 1) * 128], ciq, siq, first)
        rows.append(jnp.where(lo_head, rp, 0.0).astype(BF16))
        rows.append(jnp.where(lo_head, 0.0, rp).astype(BF16))
    qm = jnp.concatenate(rows, axis=0)

    w_t = sm_ref[0].T * ((IDX_DIM ** -0.5) * (IDX_HEADS ** -0.5))

    def score_blk(kb, c):
        lg = _nt_dot(ki_scr[kb], qm)
        s_scr[kb] = _nt_dot(ka_scr[kb], qa)
        sc = jnp.zeros((kblk, LANES), F32)
        for h in range(IDX_HEADS):
            sc = sc + jnp.maximum(lg[:, h * BLK:(h + 1) * BLK], 0.0) * w_t[SM_IW + h:SM_IW + h + 1, :]
        sc = jnp.where(sc == 0.0, 0.0, sc)
        bits = pltpu.bitcast(sc, I32)
        key = bits ^ ((bits >> 31) & 0x7FFFFFFF)
        causal = (kb * kblk + krow) <= (j * BLK + klane)
        key_scr[kb] = jnp.where(causal, key, KEY_NEG_INF)
        hi = pltpu.bitcast(bits & HI16, F32)
        khi_scr[kb] = jnp.where(causal, hi, -jnp.inf).astype(BF16)
        return c
    nk2 = (nk + 1) // 2

    def for_block_pairs(body, init):
        return lax.fori_loop(0, nk2, lambda kp, c: body(2 * kp + 1, body(2 * kp, c)), init)

    for_block_pairs(score_blk, 0)

    def count_hi(cf):
        one = jnp.ones((), BF16)
        zero = jnp.zeros((), BF16)

        def body(kp, acc):
            parts = []
            for kb in (2 * kp, 2 * kp + 1):
                m = jnp.where(khi_scr[kb] >= cf, one, zero)
                parts += [m[r * 16:(r + 1) * 16, :] for r in range(kblk // 16)]
            while len(parts) > 1:
                parts = [a + b for a, b in zip(parts[0::2], parts[1::2])]
            return acc + parts[0]
        acc = lax.fori_loop(0, nk2, body, jnp.zeros((16, LANES), BF16))
        return acc.astype(F32).sum(axis=0, keepdims=True)

    def count(pred):
        def body(kp, acc):
            for kb in (2 * kp, 2 * kp + 1):
                m = pred(key_scr[kb], kb * kblk + krow).astype(I32)
                acc = acc + m.reshape(kblk // 8, 8, LANES).sum(axis=0)
            return acc
        acc = lax.fori_loop(0, nk2, body, jnp.zeros((8, LANES), I32))
        return acc.sum(axis=0, keepdims=True)

    def search():
        def hi_step(i, tu):
            cand = tu | lax.shift_left(jnp.int32(1), 31 - i)
            cs = cand ^ INT_MIN
            fbits = (cs ^ ((cs >> 31) & 0x7FFFFFFF)) & HI16
            fbits = jnp.where((fbits > 0) & (fbits < MIN_NORMAL_BITS), MIN_NORMAL_BITS, fbits)
            cnt = count_hi(pltpu.bitcast(fbits, F32).astype(BF16))
            return jnp.where(cnt >= n_sel, cand, tu)

        def bit_step(i, tu):
            cand = tu | lax.shift_left(jnp.int32(1), 31 - i)
            cs = cand ^ INT_MIN
            cnt = count(lambda k, idx: k >= cs)
            return jnp.where(cnt >= n_sel, cand, tu)
        tu = lax.fori_loop(0, 16, hi_step, jnp.zeros((1, LANES), I32))
        tu = lax.fori_loop(16, 32, bit_step, tu)
        return tu ^ INT_MIN

    def take_all():
        return jnp.full((1, LANES), KEY_NEG_INF, I32)

    t = lax.cond((j + 1) * BLK > n_sel, search, take_all)

    def rank_blk(kb, carry):
        off, cgt = carry
        key = key_scr[kb]
        tie = jnp.where(key == t, 1.0, 0.0).astype(BF16)
        pre = _dot(tri_ref[...], tie)
        rank_scr[kb] = pre + off
        cgt = cgt + (key > t).astype(I32).reshape(kblk // 8, 8, LANES).sum(axis=0)
        return off + pre[kblk - 1:kblk, :], cgt
    _, cgt = for_block_pairs(rank_blk, (jnp.zeros((1, LANES), F32), jnp.zeros((8, LANES), I32)))
    need = jnp.where(t == KEY_NEG_INF, 0, n_sel - cgt.sum(axis=0, keepdims=True)).astype(F32)

    def mask_blk(kb, macc):
        key = key_scr[kb]
        sel = (key > t) | ((key == t) & (rank_scr[kb] <= need))
        s = s_scr[kb]
        sm = jnp.concatenate(
            [jnp.where(sel, s[:, h * BLK:(h + 1) * BLK], -jnp.inf) for h in range(A_HEADS)], axis=1)
        s_scr[kb] = sm
        return jnp.maximum(macc, sm.reshape(kblk // 8, 8, A_HEADS * BLK).max(axis=0))
    macc = for_block_pairs(mask_blk, jnp.full((8, A_HEADS * BLK), -jnp.inf, F32))
    m = macc.max(axis=0, keepdims=True)

    acc_scr[...] = jnp.zeros_like(acc_scr)

    def pv_blk(kb, lacc):
        p = jnp.exp(s_scr[kb] - m)
        acc_scr[...] += _dot(vt_scr[kb], p.astype(BF16))
        return lacc + p.reshape(kblk // 8, 8, A_HEADS * BLK).sum(axis=0)
    lacc = for_block_pairs(pv_blk, jnp.zeros((8, A_HEADS * BLK), F32))
    inv_l = 1.0 / lacc.sum(axis=0, keepdims=True)
    o_t = acc_scr[...] * inv_l
    for h in range(A_HEADS):
        y_ref[0, :, h * 128:(h + 1) * 128] = o_t[:, h * BLK:(h + 1) * BLK].T.astype(y_ref.dtype)


def _dsa_call(p3, sm3, tabs, n_sel):
    bsz, seq, _ = p3.shape
    nqb = seq // BLK
    kblk = min(KBLK, seq)
    nkb = seq // kblk
    cosa, sina, cosi, sini = tabs
    full = lambda b, j: (0, 0)
    qblk = lambda b, j: (j, 0)
    return pl.pallas_call(
        functools.partial(_dsa_body, n_sel),
        grid=(bsz, nqb),
        in_specs=[
            pl.BlockSpec((1, BLK, 512), lambda b, j: (b, j, 0)),
            pl.BlockSpec((1, seq, 512), lambda b, j: (b, 0, 1)),
            pl.BlockSpec((1, BLK, 512), lambda b, j: (b, j, 2)),
            pl.BlockSpec((1, BLK, LANES), lambda b, j: (b, j, 0)),
            pl.BlockSpec((seq, LANES), full), pl.BlockSpec((seq, LANES), full),
            pl.BlockSpec((seq, LANES), full), pl.BlockSpec((seq, LANES), full),
            pl.BlockSpec((BLK, LANES), qblk), pl.BlockSpec((BLK, LANES), qblk),
            pl.BlockSpec((BLK, LANES), qblk), pl.BlockSpec((BLK, LANES), qblk),
            pl.BlockSpec((kblk, kblk), full),
        ],
        out_specs=pl.BlockSpec((1, BLK, A_W), lambda b, j: (b, j, 0)),
        out_shape=jax.ShapeDtypeStruct((bsz, seq, A_W), BF16),
        scratch_shapes=[
            pltpu.VMEM((nkb, kblk, 128), BF16),
            pltpu.VMEM((nkb, kblk, 128), BF16),
            pltpu.VMEM((nkb, 128, kblk), BF16),
            pltpu.VMEM((nkb, kblk, LANES), I32),
            pltpu.VMEM((nkb, kblk, LANES), BF16),
            pltpu.VMEM((nkb, kblk, LANES), F32),
            pltpu.VMEM((nkb, kblk, A_HEADS * BLK), F32),
            pltpu.VMEM((128, A_HEADS * BLK), F32),
        ],
        compiler_params=pltpu.CompilerParams(
            dimension_semantics=("arbitrary", "arbitrary"), vmem_limit_bytes=VMEM_LIMIT),
        name="dsa",
    )(p3, p3, p3, sm3, cosa, sina, cosi, sini, cosa, sina, cosi, sini, jnp.tril(jnp.ones((kblk, kblk), BF16)))


def _gdn_body(q_ref, k_ref, v_ref, z_ref, sm_ref, cw_ref, alog_ref, dtb_ref, gn_ref,
              y_ref, s_scr):
    c = GDN_BLK
    hc = GDN_CHUNK
    nc = q_ref.shape[1] // c
    row = lax.broadcasted_iota(I32, (c, LANES), 0)
    col = lax.broadcasted_iota(I32, (c, LANES), 1)

    def same_block(size):
        sh = int(math.log2(size))
        return (row >> sh) == (col >> sh)

    same_chunk = same_block(hc)
    incl = same_chunk & (row >= col)
    strict = same_chunk & (row > col)
    eye = (row == col).astype(F32)
    sizes = [GDN_BASE * 2 ** i for i in range(int(math.log2(hc // GDN_BASE)) + 1)]
    base_mask = same_block(GDN_BASE) & (row > col)
    level_masks = [same_block(big) & jnp.logical_not(same_block(small))
                   for small, big in zip(sizes[:-1], sizes[1:])]
    top_rows = row < hc
    s_scr[...] = jnp.zeros_like(s_scr)
    row8 = lax.broadcasted_iota(I32, (8, B_W), 0)

    def split(a):
        hi = a.astype(BF16)
        return hi, (a - hi.astype(F32)).astype(BF16)

    def mm3(a, b):
        ah, al = a
        bh, bl = b
        r = _dot(ah, bh)
        if bl is not None:
            r = r + _dot(ah, bl)
        if al is not None:
            r = r + _dot(al, bh)
        return r

    def unit_lower_inverse(ms, fillers):
        fillers = list(fillers)

        def fill():
            if fillers:
                fillers.pop(0)()
        n1 = [(jnp.where(base_mask, -m, 0.0).astype(BF16), None) for m in ms]
        d = [eye + x[0].astype(F32) for x in n1]
        pw = n1
        for _ in range(int(math.log2(GDN_BASE)) - 1):
            pw = [split(mm3(x, x)) for x in pw]
            fill()
            d = [x + mm3(split(x), p) for x, p in zip(d, pw)]
        for lm in level_masks:
            ds = [split(x) for x in d]
            t = [split(mm3(y, (jnp.where(lm, m, 0.0).astype(BF16), None))) for y, m in zip(ds, ms)]
            fill()
            d = [x - mm3(tt, y) for x, tt, y in zip(d, t, ds)]
        while fillers:
            fill()
        return d

    def conv_silu(ref, n, wofs):
        r0 = pl.multiple_of(n * c, c)
        cur = ref[0, pl.ds(r0, c), :].astype(F32)
        pr0 = pl.multiple_of(jnp.maximum(n, 1) * c - 16, 16)
        tail = ref[0, pl.ds(pr0, 16), :].astype(F32) * jnp.where(n > 0, 1.0, 0.0)
        y = cur * cw_ref[CONV_WIDTH - 1:CONV_WIDTH, wofs:wofs + B_W]
        for s in range(1, CONV_WIDTH):
            sh = pltpu.roll(cur, s, 0)
            top = jnp.where(row8 < s, pltpu.roll(tail, s, 0)[:8], sh[:8])
            sh = jnp.concatenate([top, sh[8:]], axis=0)
            y = y + sh * cw_ref[CONV_WIDTH - 1 - s:CONV_WIDTH - s, wofs:wofs + B_W]
        return _silu(y)

    def prep(n):
        r0 = pl.multiple_of(n * c, c)
        kc = conv_silu(k_ref, n, B_W)
        sm = sm_ref[0, pl.ds(r0, c), :]
        g = -jnp.exp(alog_ref[...]) * _softplus(sm + dtb_ref[...])
        beta = _sigmoid(sm)
        s = 1
        while s < hc:
            g = g + jnp.where((row & (hc - 1)) >= s, pltpu.roll(g, s, 0), 0.0)
            s *= 2
        g_t = g.T
        heads = range(B_HEADS)
        sls = [slice(h * 128, (h + 1) * 128) for h in heads]
        gcol = [jnp.broadcast_to(g[:, SM_BA + h:SM_BA + h + 1], (c, LANES)) for h in heads]
        grow = [jnp.broadcast_to(g_t[SM_BA + h:SM_BA + h + 1, :], (c, LANES)) for h in heads]
        bcol = [jnp.broadcast_to(beta[:, SM_BB + h:SM_BB + h + 1], (c, LANES)) for h in heads]
        kh = [kc[:, sl] for sl in sls]
        kh = [x * lax.rsqrt(jnp.sum(x * x, axis=-1, keepdims=True) + EPS) for x in kh]
        khb = [x.astype(BF16) for x in kh]
        decay = [jnp.exp(jnp.where(incl, gc - gr, -jnp.inf)) for gc, gr in zip(gcol, grow)]
        kbeta = [k * b for k, b in zip(kh, bcol)]
        ms = [jnp.where(strict, _nt_dot(kb.astype(BF16), k) * dc, 0.0) for kb, k, dc in zip(kbeta, khb, decay)]
        egc = [jnp.exp(x) for x in gcol]
        env = {}

        def fill_q():
            qc = conv_silu(q_ref, n, 0)
            qh = [qc[:, sl] for sl in sls]
            env["qh"] = [x * lax.rsqrt(jnp.sum(x * x, axis=-1, keepdims=True) + EPS) * (B_HEAD_DIM ** -0.5)
                         for x in qh]

        def fill_v():
            vc = conv_silu(v_ref, n, 2 * B_W)
            env["rhs"] = [jnp.concatenate([(vc[:, sl] * b).astype(BF16), (kb * e).astype(BF16)], axis=1)
                          for sl, b, kb, e in zip(sls, bcol, kbeta, egc)]

        def fill_qk():
            env["a_intra"] = [(_nt_dot(q.astype(BF16), k) * dc).astype(BF16)
                              for q, k, dc in zip(env["qh"], khb, decay)]
            env["qg"] = [(q * e).astype(BF16) for q, e in zip(env["qh"], egc)]

        def finish(inv):
            a_intra, qg, rhs = env["a_intra"], env["qg"], env["rhs"]
            uw = [_dot(i.astype(BF16), r) for i, r in zip(inv, rhs)]
            u = [x[:, :128] for x in uw]
            wb = [x[:, 128:].astype(BF16) for x in uw]
            kd_t = [(k * jnp.exp(jnp.where(top_rows, gc[hc - 1:hc, :], gc[c - 1:c, :]) - gc)).T
                    for k, gc in zip(kh, gcol)]
            st = [s_scr[h] for h in heads]
            o_parts = [[] for _ in heads]
            for ci in range(c // hc):
                rs = slice(ci * hc, (ci + 1) * hc)
                in_chunk = (col >> int(math.log2(hc))) == ci
                stb = [x.astype(BF16) for x in st]
                v_new = [u[h][rs] - _dot(wb[h][rs], stb[h]) for h in heads]
                vn_pad = [jnp.concatenate([x] * (c // hc), axis=0).astype(BF16) for x in v_new]
                for h in heads:
                    o_parts[h].append(_dot(qg[h][rs], stb[h]) + _dot(a_intra[h][rs], vn_pad[h]))
                st = [st[h] * jnp.exp(gcol[h][(ci + 1) * hc - 1:(ci + 1) * hc, :])
                      + _dot(jnp.where(in_chunk, kd_t[h], 0.0).astype(BF16), vn_pad[h]) for h in heads]
            for h in heads:
                s_scr[h] = st[h]
                o = jnp.concatenate(o_parts[h], axis=0)
                zo = z_ref[0, pl.ds(r0, c), sls[h]].astype(F32)
                on = o * lax.rsqrt(jnp.mean(o * o, axis=-1, keepdims=True) + EPS) * gn_ref[...]
                y_ref[0, pl.ds(r0, c), sls[h]] = (on * _silu(zo)).astype(y_ref.dtype)

        return ms, [fill_q, fill_v, fill_qk], finish

    def step(i, carry):
        parts = [prep(GDN_UNROLL * i + u) for u in range(GDN_UNROLL)]
        ms = [m for p in parts for m in p[0]]
        fillers = [f for group in zip(*[p[1] for p in parts]) for f in group]
        inv = unit_lower_inverse(ms, fillers)
        for u, p in enumerate(parts):
            p[2](inv[u * B_HEADS:(u + 1) * B_HEADS])
        return carry

    assert nc % GDN_UNROLL == 0
    lax.fori_loop(0, nc // GDN_UNROLL, step, 0)


def _gdn_call(p3, sm3, conv_w, alog_v, dtb_v, gnorm):
    bsz, seq, _ = p3.shape
    col = lambda g: (lambda b: (b, 0, g))
    vec = lambda b: (0, 0)
    return pl.pallas_call(
        _gdn_body,
        grid=(bsz,),
        in_specs=[
            pl.BlockSpec((1, seq, B_W), col(3)), pl.BlockSpec((1, seq, B_W), col(4)),
            pl.BlockSpec((1, seq, B_W), col(5)), pl.BlockSpec((1, seq, B_W), col(6)),
            pl.BlockSpec((1, seq, LANES), lambda b: (b, 0, 0)),
            pl.BlockSpec((CONV_WIDTH, 3 * B_W), vec),
            pl.BlockSpec((1, LANES), vec), pl.BlockSpec((1, LANES), vec), pl.BlockSpec((1, LANES), vec),
        ],
        out_specs=pl.BlockSpec((1, seq, B_W), lambda b: (b, 0, 0)),
        out_shape=jax.ShapeDtypeStruct((bsz, seq, B_W), BF16),
        scratch_shapes=[pltpu.VMEM((B_HEADS, 128, 128), F32)],
        compiler_params=pltpu.CompilerParams(
            dimension_semantics=("arbitrary",), vmem_limit_bytes=VMEM_LIMIT),
        name="gdn",
    )(p3, p3, p3, p3, sm3, conv_w, alog_v, dtb_v, gnorm)


def _ret_body(qk_ref, v_ref, g_ref, cos_ref, sin_ref, dmat_ref, xi_ref, zeta_ref, gch_ref,
              y_ref, s_scr):
    c = RET_BLK
    nc = v_ref.shape[1] // c
    lane = lax.broadcasted_iota(I32, (c, LANES), 1)
    first = (lane & 32) == 0
    lo_head = lane < 64
    s_scr[...] = jnp.zeros_like(s_scr)

    def chunk(n, carry):
        r0 = pl.multiple_of(n * c, c)
        qk = qk_ref[0, pl.ds(r0, c), :].astype(F32)
        cs = cos_ref[pl.ds(r0, c), :]
        sn = sin_ref[pl.ds(r0, c), :]
        for p in range(C_HEADS // 2):
            qp = _rot_half32(qk[:, p * 128:(p + 1) * 128], cs, sn, first)
            kp = _rot_half32(qk[:, C_QK + p * 128:C_QK + (p + 1) * 128], cs, sn, first) * (C_KEY_DIM ** -0.5)
            qm = [jnp.where(lo_head, qp, 0.0).astype(BF16), jnp.where(lo_head, 0.0, qp).astype(BF16)]
            km = [jnp.where(lo_head, kp, 0.0), jnp.where(lo_head, 0.0, kp)]
            sc = _nt_dot(jnp.concatenate(qm, axis=0), kp.astype(BF16))
            for e in range(2):
                h = 2 * p + e
                sl = slice(h * 128, (h + 1) * 128)
                vh = v_ref[0, pl.ds(r0, c), sl].astype(F32)
                scores = sc[e * c:(e + 1) * c] * dmat_ref[h]
                st = s_scr[h]
                o = _dot(scores.astype(BF16), vh.astype(BF16)) + _dot(qm[e], st.astype(BF16)) * xi_ref[h]
                s_scr[h] = st * gch_ref[h, 0:1, :] + _dot(km[e].T.astype(BF16), (vh * zeta_ref[h]).astype(BF16))
                on = o * lax.rsqrt(jnp.mean(o * o, axis=-1, keepdims=True) + EPS)
                go = g_ref[0, pl.ds(r0, c), sl].astype(F32)
                y_ref[0, pl.ds(r0, c), sl] = (on * _silu(go)).astype(y_ref.dtype)
        return carry

    assert nc % 2 == 0
    lax.fori_loop(0, nc // 2, lambda i, carry: chunk(2 * i + 1, chunk(2 * i, carry)), 0)


def _ret_call(p3, cos_t, sin_t, dmat, xi, zeta, gch):
    bsz, seq, _ = p3.shape
    c = RET_BLK
    z2 = lambda b: (0, 0)
    z3 = lambda b: (0, 0, 0)
    return pl.pallas_call(
        _ret_body,
        grid=(bsz,),
        in_specs=[
            pl.BlockSpec((1, seq, 512), lambda b: (b, 0, 7)),
            pl.BlockSpec((1, seq, 512), lambda b: (b, 0, 8)),
            pl.BlockSpec((1, seq, 512), lambda b: (b, 0, 9)),
            pl.BlockSpec((seq, LANES), z2), pl.BlockSpec((seq, LANES), z2),
            pl.BlockSpec((C_HEADS, c, c), z3),
            pl.BlockSpec((C_HEADS, c, LANES), z3), pl.BlockSpec((C_HEADS, c, LANES), z3),
            pl.BlockSpec((C_HEADS, 8, LANES), z3),
        ],
        out_specs=pl.BlockSpec((1, seq, C_W), lambda b: (b, 0, 0)),
        out_shape=jax.ShapeDtypeStruct((bsz, seq, C_W), BF16),
        scratch_shapes=[pltpu.VMEM((C_HEADS, 128, 128), F32)],
        compiler_params=pltpu.CompilerParams(
            dimension_semantics=("arbitrary",), vmem_limit_bytes=VMEM_LIMIT),
        name="ret",
    )(p3, p3, p3, cos_t, sin_t, dmat, xi, zeta, gch)


def _merge_body(ya_ref, yb_ref, yc_ref, g0_ref, g1_ref, g2_ref, x_ref, wb_ref, wo_ref, ln_ref, o_ref):
    def gate(ref):
        return _sigmoid(ref[...].astype(F32))
    merged = (gate(g0_ref) * _dot(ya_ref[...], wb_ref[0:A_W, :])
              + gate(g1_ref) * _dot(yb_ref[...], wb_ref[A_W:A_W + B_W, :])
              + gate(g2_ref) * _dot(yc_ref[...], wb_ref[A_W + B_W:, :]))
    o = _dot(merged.astype(BF16), wo_ref[...])
    on = o * lax.rsqrt(jnp.mean(o * o, axis=-1, keepdims=True) + EPS) * ln_ref[...]
    o_ref[...] = x_ref[...] + on


def _merge_call(ya, yb, yc, p2d, x2d, wb, wo, ln):
    n = x2d.shape[0]
    tm = min(512, n)
    rowb = lambda i: (i, 0)
    cst = lambda i: (0, 0)
    gcol = lambda g: (lambda i: (i, g))
    return pl.pallas_call(
        _merge_body,
        grid=(n // tm,),
        in_specs=[
            pl.BlockSpec((tm, A_W), rowb), pl.BlockSpec((tm, B_W), rowb), pl.BlockSpec((tm, C_W), rowb),
            pl.BlockSpec((tm, D_MODEL), gcol(5)), pl.BlockSpec((tm, D_MODEL), gcol(6)),
            pl.BlockSpec((tm, D_MODEL), gcol(7)),
            pl.BlockSpec((tm, D_MODEL), rowb),
            pl.BlockSpec((A_W + B_W + C_W, D_MODEL), cst),
            pl.BlockSpec((D_MODEL, D_MODEL), cst),
            pl.BlockSpec((1, D_MODEL), cst),
        ],
        out_specs=pl.BlockSpec((tm, D_MODEL), rowb),
        out_shape=jax.ShapeDtypeStruct((n, D_MODEL), F32),
        compiler_params=pltpu.CompilerParams(
            dimension_semantics=("arbitrary",), vmem_limit_bytes=VMEM_LIMIT),
        name="merge",
    )(ya, yb, yc, p2d, p2d, p2d, x2d, wb, wo, ln)


def _mlp_body(x_ref, lnpre_ref, wu_ref, wd_ref, lnpost_ref, o_ref):
    x = x_ref[...]
    h = (x * lax.rsqrt(jnp.mean(x * x, axis=-1, keepdims=True) + EPS) * lnpre_ref[...]).astype(BF16)
    ff = jnp.zeros(x.shape, F32)
    for cidx in range(D_FF // D_MODEL):
        sl = slice(cidx * D_MODEL, (cidx + 1) * D_MODEL)
        u = jnp.maximum(_dot(h, wu_ref[:, sl]), 0.0)
        ff = ff + _dot((u * u).astype(BF16), wd_ref[sl, :])
    o_ref[...] = x + ff * lax.rsqrt(jnp.mean(ff * ff, axis=-1, keepdims=True) + EPS) * lnpost_ref[...]


def _mlp_call(x2d, lnpre, wu, wd, lnpost):
    n = x2d.shape[0]
    tm = min(512, n)
    rowb = lambda i: (i, 0)
    cst = lambda i: (0, 0)
    return pl.pallas_call(
        _mlp_body,
        grid=(n // tm,),
        in_specs=[
            pl.BlockSpec((tm, D_MODEL), rowb),
            pl.BlockSpec((1, D_MODEL), cst),
            pl.BlockSpec((D_MODEL, D_FF), cst),
            pl.BlockSpec((D_FF, D_MODEL), cst),
            pl.BlockSpec((1, D_MODEL), cst),
        ],
        out_specs=pl.BlockSpec((tm, D_MODEL), rowb),
        out_shape=jax.ShapeDtypeStruct((n, D_MODEL), F32),
        compiler_params=pltpu.CompilerParams(
            dimension_semantics=("arbitrary",), vmem_limit_bytes=VMEM_LIMIT),
        name="mlp",
    )(x2d, lnpre, wu, wd, lnpost)


def _rope_tables(seq):
    pos = jnp.arange(seq, dtype=F32)
    inv_a = ROPE_THETA ** (-jnp.arange(0, A_HEAD_DIM, 2, dtype=F32) / A_HEAD_DIM)
    ang_a = pos[:, None] * inv_a[None, :]
    cosa = jnp.concatenate([jnp.cos(ang_a)] * 2, axis=1)
    sina = jnp.concatenate([-jnp.sin(ang_a), jnp.sin(ang_a)], axis=1)
    inv_i = ROPE_THETA ** (-jnp.arange(0, IDX_DIM, 2, dtype=F32) / IDX_DIM)
    ang_i = pos[:, None] * inv_i[None, :]
    cosi = jnp.concatenate([jnp.cos(ang_i)] * 4, axis=1)
    sini = jnp.concatenate([-jnp.sin(ang_i), jnp.sin(ang_i)] * 2, axis=1)
    inv_c = 1.0 / (ROPE_THETA ** jnp.linspace(0.0, 1.0, C_KEY_DIM // 2, dtype=F32))
    ang_c = pos[:, None] * inv_c[None, :]
    cosc = jnp.concatenate([jnp.cos(ang_c)] * 4, axis=1)
    sinc = jnp.concatenate([-jnp.sin(ang_c), jnp.sin(ang_c)] * 2, axis=1)
    return (cosa, sina, cosi, sini), (cosc, sinc)


def _ret_tables():
    c = RET_BLK
    log_gamma = jnp.log(1.0 - 2.0 ** (-5.0 - jnp.arange(C_HEADS, dtype=F32)))
    r = jnp.arange(c, dtype=F32)
    rel = r[:, None] - r[None, :]
    dmat = jnp.where(rel >= 0, jnp.exp(jnp.maximum(rel, 0.0)[None] * log_gamma[:, None, None]), 0.0)
    xi = jnp.exp((r + 1.0)[None] * log_gamma[:, None])
    zeta = jnp.exp((c - 1.0 - r)[None] * log_gamma[:, None])
    gch = jnp.exp(c * log_gamma)
    xi_b = jnp.broadcast_to(xi[:, :, None], (C_HEADS, c, LANES))
    zeta_b = jnp.broadcast_to(zeta[:, :, None], (C_HEADS, c, LANES))
    gch_b = jnp.broadcast_to(gch[:, None, None], (C_HEADS, 8, LANES))
    return dmat, xi_b, zeta_b, gch_b


def _lane_vec(vals, ofs):
    v = jnp.zeros((1, LANES), F32)
    return v.at[0, ofs:ofs + vals.shape[0]].set(vals.astype(F32))


def kernel(x, ln_mix_pre, w_in, gdn_conv, gdn_a_log, gdn_dt_bias, gdn_norm, w_branch, w_out,
           ln_mix_post, ln_mlp_pre, w_up, w_down, ln_mlp_post):
    bsz, seq, _ = x.shape
    depth = w_in.shape[0]
    n = bsz * seq
    n_sel = min(TOPK_MAX, seq // 4)
    assert seq % BLK == 0 and n_sel % BLK == 0
    assert seq <= 16 * 256
    assert seq % (2 * KBLK) == 0
    dsa_tabs, (cosc, sinc) = _rope_tables(seq)
    dmat, xi_b, zeta_b, gch_b = _ret_tables()

    x2d = x.reshape(n, D_MODEL)
    for l in range(depth):
        w_big, w_small = _prep_w_in(w_in[l])
        p2d, sm2d = _proj_call(x2d, ln_mix_pre[l][None, :], w_big, w_small)
        p3 = p2d.reshape(bsz, seq, P_WIDTH)
        sm3 = sm2d.reshape(bsz, seq, LANES)
        y_a = _dsa_call(p3, sm3, dsa_tabs, n_sel)
        y_b = _gdn_call(p3, sm3, gdn_conv[l], _lane_vec(gdn_a_log[l], SM_BA),
                        _lane_vec(gdn_dt_bias[l], SM_BA), gdn_norm[l][None, :])
        y_c = _ret_call(p3, cosc, sinc, dmat, xi_b, zeta_b, gch_b)
        x2d = _merge_call(y_a.reshape(n, A_W), y_b.reshape(n, B_W), y_c.reshape(n, C_W), p2d, x2d,
                          w_branch[l].astype(BF16), w_out[l].astype(BF16), ln_mix_post[l][None, :])
        x2d = _mlp_call(x2d, ln_mlp_pre[l][None, :], w_up[l].astype(BF16), w_down[l].astype(BF16),
                        ln_mlp_post[l][None, :])
    return x2d.reshape(bsz, seq, D_MODEL)
```

```python
import functools
import math

import numpy as np
import jax
import jax.numpy as jnp
from jax import lax
from jax.experimental import pallas as pl
from jax.experimental.pallas import tpu as pltpu

F32 = jnp.float32
BF16 = jnp.bfloat16
I32 = jnp.int32

D_MODEL = 1024
A_HEADS = 4
A_HEAD_DIM = 128
IDX_HEADS = 8
IDX_DIM = 64
TOPK_MAX = 256
B_HEADS = 4
B_HEAD_DIM = 128
CONV_WIDTH = 4
C_HEADS = 4
C_KEY_DIM = 64
C_VAL_DIM = 128
D_FF = 4 * D_MODEL
ROPE_THETA = 10000.0
EPS = 1e-6
N_BRANCH = 3
A_W = A_HEADS * A_HEAD_DIM
B_W = B_HEADS * B_HEAD_DIM
C_W = C_HEADS * C_VAL_DIM
C_QK = C_HEADS * C_KEY_DIM

LANES = 128
BLK = 128
KBLK = 256
GDN_BLK = 128
GDN_CHUNK = 64
GDN_BASE = 8
GDN_UNROLL = 4
RET_BLK = 128
RET_UNROLL = 4
TM_DENSE = 1024
VMEM_LIMIT = 52 * 1024 * 1024

_OFF = {}
_o = 0
for _name, _w in (("a_q", A_W), ("a_k", A_HEAD_DIM), ("a_v", A_HEAD_DIM),
                  ("i_q", IDX_HEADS * IDX_DIM), ("i_k", IDX_DIM), ("i_w", IDX_HEADS),
                  ("b_q", B_W), ("b_k", B_W), ("b_v", B_W), ("b_a", B_HEADS), ("b_b", B_HEADS),
                  ("b_z", B_W), ("c_q", C_QK), ("c_k", C_QK), ("c_v", C_W), ("c_g", C_W),
                  ("gate", N_BRANCH * D_MODEL)):
    _OFF[_name] = (_o, _w)
    _o += _w

P_WIDTH = 8192
P_TILE = 1024
SM_IW = 0
SM_BA = 8
SM_BB = 12

KEY_NEG_INF = -2139095041
INT_MIN = -2147483648


def _cols(w, name):
    o, n = _OFF[name]
    return w[:, o:o + n]


def _prep_w_in(w):
    ik = _cols(w, "i_k")
    big = jnp.concatenate([
        _cols(w, "a_q"), _cols(w, "a_k"), _cols(w, "a_v"), ik, ik, ik, ik,
        _cols(w, "i_q"),
        _cols(w, "b_q"), _cols(w, "b_k"), _cols(w, "b_v"), _cols(w, "b_z"),
        _cols(w, "c_q"), _cols(w, "c_k"), _cols(w, "c_v"), _cols(w, "c_g"),
        _cols(w, "gate")], axis=1)
    assert big.shape[1] == P_WIDTH
    small = jnp.concatenate([
        _cols(w, "i_w"), _cols(w, "b_a"), _cols(w, "b_b"),
        jnp.zeros((w.shape[0], LANES - 16), w.dtype)], axis=1)
    return big.astype(BF16), small.astype(BF16)


def _nt_dot(a, b):
    return lax.dot_general(a, b, (((1,), (1,)), ((), ())), preferred_element_type=F32)


def _dot(a, b):
    return jnp.dot(a, b, preferred_element_type=F32)


def _sigmoid(x):
    return 0.5 + 0.5 * jnp.tanh(0.5 * x)


def _silu(x):
    return x * _sigmoid(x)


def _softplus(x):
    return jnp.maximum(x, 0.0) + jnp.log(1.0 + jnp.exp(-jnp.abs(x)))


def _rot_half64(x, cos, sin_signed):
    return x * cos + pltpu.roll(x, 64, 1) * sin_signed


def _rot_half32(x, cos, sin_signed, first):
    partner = jnp.where(first, pltpu.roll(x, 96, 1), pltpu.roll(x, 32, 1))
    return x * cos + partner * sin_signed


def _proj_body(x_ref, gain_ref, w_ref, ws_ref, p_ref, s_ref):
    x = x_ref[...]
    ms = jnp.mean(x * x, axis=-1, keepdims=True)
    h = (x * lax.rsqrt(ms + EPS) * gain_ref[...]).astype(BF16)
    s_ref[...] = _dot(h, ws_ref[...])
    for t in range(P_WIDTH // P_TILE):
        sl = slice(t * P_TILE, (t + 1) * P_TILE)
        p_ref[:, sl] = _dot(h, w_ref[:, sl]).astype(p_ref.dtype)


def _proj_call(x2d, gain, w_big, w_small):
    n = x2d.shape[0]
    tm = min(512, n)
    once = pl.Buffered(1)
    return pl.pallas_call(
        _proj_body,
        grid=(n // tm,),
        in_specs=[
            pl.BlockSpec((tm, D_MODEL), lambda i: (i, 0)),
            pl.BlockSpec((1, D_MODEL), lambda i: (0, 0)),
            pl.BlockSpec((D_MODEL, P_WIDTH), lambda i: (0, 0), pipeline_mode=once),
            pl.BlockSpec((D_MODEL, LANES), lambda i: (0, 0), pipeline_mode=once),
        ],
        out_specs=[
            pl.BlockSpec((tm, P_WIDTH), lambda i: (i, 0)),
            pl.BlockSpec((tm, LANES), lambda i: (i, 0)),
        ],
        out_shape=[
            jax.ShapeDtypeStruct((n, P_WIDTH), BF16),
            jax.ShapeDtypeStruct((n, LANES), F32),
        ],
        compiler_params=pltpu.CompilerParams(
            dimension_semantics=("arbitrary",), vmem_limit_bytes=VMEM_LIMIT),
        name="proj",
    )(x2d, gain, w_big, w_small)


def _dsa_body(n_sel, aq_ref, kv_ref, iq_ref, sm_ref,
              cosa_ref, sina_ref, cosi_ref, sini_ref,
              cosaq_ref, sinaq_ref, cosiq_ref, siniq_ref, tri_ref,
              y_ref, ka_scr, ki_scr, vt_scr, key_scr, rank_scr, s_scr, acc_scr):
    j = pl.program_id(1)
    nkb = ka_scr.shape[0]
    kblk = ka_scr.shape[1]
    seq = nkb * kblk
    nk = (j * BLK + BLK + kblk - 1) // kblk
    lane = lax.broadcasted_iota(I32, (BLK, LANES), 1)
    klane = lax.broadcasted_iota(I32, (kblk, LANES), 1)
    krow = lax.broadcasted_iota(I32, (kblk, LANES), 0)
    first = (lane & 32) == 0
    kfirst = (klane & 32) == 0
    lo_head = lane < 64

    @pl.when(j == 0)
    def _():
        def prep(kb, c):
            r0 = pl.multiple_of(kb * kblk, kblk)
            kv = kv_ref[0, pl.ds(r0, kblk), :].astype(F32)
            ca = cosa_ref[pl.ds(r0, kblk), :]
            sa = sina_ref[pl.ds(r0, kblk), :]
            ci = cosi_ref[pl.ds(r0, kblk), :]
            si = sini_ref[pl.ds(r0, kblk), :]
            ka_scr[kb] = _rot_half64(kv[:, 0:128], ca, sa).astype(BF16)
            ki_scr[kb] = _rot_half32(kv[:, 256:384], ci, si, kfirst).astype(BF16)
            vt_scr[kb] = kv[:, 128:256].T.astype(BF16)
            return c
        lax.fori_loop(0, nkb, prep, 0)

    aq = aq_ref[0].astype(F32)
    caq = cosaq_ref[...]
    saq = sinaq_ref[...]
    qa = jnp.concatenate(
        [(_rot_half64(aq[:, h * 128:(h + 1) * 128], caq, saq) * (A_HEAD_DIM ** -0.5)).astype(BF16)
         for h in range(A_HEADS)], axis=0)

    iq = iq_ref[0].astype(F32)
    ciq = cosiq_ref[...]
    siq = siniq_ref[...]
    rows = []
    for p in range(IDX_HEADS // 2):
        rp = _rot_half32(iq[:, p * 128:(p + 1) * 128], ciq, siq, first)
        rows.append(jnp.where(lo_head, rp, 0.0).astype(BF16))
        rows.append(jnp.where(lo_head, 0.0, rp).astype(BF16))
    qm = jnp.concatenate(rows, axis=0)

    w_t = sm_ref[0].T * ((IDX_DIM ** -0.5) * (IDX_HEADS ** -0.5))

    def score_blk(kb, c):
        lg = _nt_dot(ki_scr[kb], qm)
        s_scr[kb] = _nt_dot(ka_scr[kb], qa)
        sc = jnp.zeros((kblk, LANES), F32)
        for h in range(IDX_HEADS):
            sc = sc + jnp.maximum(lg[:, h * BLK:(h + 1) * BLK], 0.0) * w_t[SM_IW + h:SM_IW + h + 1, :]
        sc = jnp.where(sc == 0.0, 0.0, sc)
        bits = pltpu.bitcast(sc, I32)
        key = bits ^ ((bits >> 31) & 0x7FFFFFFF)
        causal = (kb * kblk + krow) <= (j * BLK + klane)
        key_scr[kb] = jnp.where(causal, key, KEY_NEG_INF)
        return c
    nk2 = (nk + 1) // 2

    def for_block_pairs(body, init):
        return lax.fori_loop(0, nk2, lambda kp, c: body(2 * kp + 1, body(2 * kp, c)), init)

    for_block_pairs(score_blk, 0)

    def count(pred):
        def body(kp, acc):
            for kb in (2 * kp, 2 * kp + 1):
                m = pred(key_scr[kb], kb * kblk + krow).astype(I32)
                acc = acc + m.reshape(kblk // 8, 8, LANES).sum(axis=0)
            return acc
        acc = lax.fori_loop(0, nk2, body, jnp.zeros((8, LANES), I32))
        return acc.sum(axis=0, keepdims=True)

    def search():
        def bit_step(i, tu):
            cand = tu | lax.shift_left(jnp.int32(1), 31 - i)
            cs = cand ^ INT_MIN
            cnt = count(lambda k, idx: k >= cs)
            return jnp.where(cnt >= n_sel, cand, tu)
        tu = lax.fori_loop(0, 32, bit_step, jnp.zeros((1, LANES), I32))
        return tu ^ INT_MIN

    def take_all():
        return jnp.full((1, LANES), KEY_NEG_INF, I32)

    t = lax.cond((j + 1) * BLK > n_sel, search, take_all)

    def rank_blk(kb, carry):
        off, cgt = carry
        key = key_scr[kb]
        tie = jnp.where(key == t, 1.0, 0.0).astype(BF16)
        pre = _dot(tri_ref[...], tie)
        rank_scr[kb] = pre + off
        cgt = cgt + (key > t).astype(I32).reshape(kblk // 8, 8, LANES).sum(axis=0)
        return off + pre[kblk - 1:kblk, :], cgt
    _, cgt = for_block_pairs(rank_blk, (jnp.zeros((1, LANES), F32), jnp.zeros((8, LANES), I32)))
    need = jnp.where(t == KEY_NEG_INF, 0, n_sel - cgt.sum(axis=0, keepdims=True)).astype(F32)

    def mask_blk(kb, macc):
        key = key_scr[kb]
        sel = (key > t) | ((key == t) & (rank_scr[kb] <= need))
        s = s_scr[kb]
        sm = jnp.concatenate(
            [jnp.where(sel, s[:, h * BLK:(h + 1) * BLK], -jnp.inf) for h in range(A_HEADS)], axis=1)
        s_scr[kb] = sm
        return jnp.maximum(macc, sm.reshape(kblk // 8, 8, A_HEADS * BLK).max(axis=0))
    macc = for_block_pairs(mask_blk, jnp.full((8, A_HEADS * BLK), -jnp.inf, F32))
    m = macc.max(axis=0, keepdims=True)

    acc_scr[...] = jnp.zeros_like(acc_scr)

    def pv_blk(kb, lacc):
        p = jnp.exp(s_scr[kb] - m)
        acc_scr[...] += _dot(vt_scr[kb], p.astype(BF16))
        return lacc + p.reshape(kblk // 8, 8, A_HEADS * BLK).sum(axis=0)
    lacc = for_block_pairs(pv_blk, jnp.zeros((8, A_HEADS * BLK), F32))
    inv_l = 1.0 / lacc.sum(axis=0, keepdims=True)
    o_t = acc_scr[...] * inv_l
    for h in range(A_HEADS):
        y_ref[0, :, h * 128:(h + 1) * 128] = o_t[:, h * BLK:(h + 1) * BLK].T.astype(y_ref.dtype)


def _dsa_call(p3, sm3, tabs, n_sel):
    bsz, seq, _ = p3.shape
    nqb = seq // BLK
    kblk = min(KBLK, seq)
    nkb = seq // kblk
    cosa, sina, cosi, sini = tabs
    full = lambda b, j: (0, 0)
    qblk = lambda b, j: (j, 0)
    return pl.pallas_call(
        functools.partial(_dsa_body, n_sel),
        grid=(bsz, nqb),
        in_specs=[
            pl.BlockSpec((1, BLK, 512), lambda b, j: (b, j, 0)),
            pl.BlockSpec((1, seq, 512), lambda b, j: (b, 0, 1)),
            pl.BlockSpec((1, BLK, 512), lambda b, j: (b, j, 2)),
            pl.BlockSpec((1, BLK, LANES), lambda b, j: (b, j, 0)),
            pl.BlockSpec((seq, LANES), full), pl.BlockSpec((seq, LANES), full),
            pl.BlockSpec((seq, LANES), full), pl.BlockSpec((seq, LANES), full),
            pl.BlockSpec((BLK, LANES), qblk), pl.BlockSpec((BLK, LANES), qblk),
            pl.BlockSpec((BLK, LANES), qblk), pl.BlockSpec((BLK, LANES), qblk),
            pl.BlockSpec((kblk, kblk), full),
        ],
        out_specs=pl.BlockSpec((1, BLK, A_W), lambda b, j: (b, j, 0)),
        out_shape=jax.ShapeDtypeStruct((bsz, seq, A_W), BF16),
        scratch_shapes=[
            pltpu.VMEM((nkb, kblk, 128), BF16),
            pltpu.VMEM((nkb, kblk, 128), BF16),
            pltpu.VMEM((nkb, 128, kblk), BF16),
            pltpu.VMEM((nkb, kblk, LANES), I32),
            pltpu.VMEM((nkb, kblk, LANES), F32),
            pltpu.VMEM((nkb, kblk, A_HEADS * BLK), F32),
            pltpu.VMEM((128, A_HEADS * BLK), F32),
        ],
        compiler_params=pltpu.CompilerParams(
            dimension_semantics=("arbitrary", "arbitrary"), vmem_limit_bytes=VMEM_LIMIT),
        name="dsa",
    )(p3, p3, p3, sm3, cosa, sina, cosi, sini, cosa, sina, cosi, sini, jnp.tril(jnp.ones((kblk, kblk), BF16)))


def _gdn_body(q_ref, k_ref, v_ref, z_ref, sm_ref, cw_ref, alog_ref, dtb_ref, gn_ref,
              y_ref, s_scr):
    c = GDN_BLK
    hc = GDN_CHUNK
    nc = q_ref.shape[1] // c
    row = lax.broadcasted_iota(I32, (c, LANES), 0)
    col = lax.broadcasted_iota(I32, (c, LANES), 1)

    def same_block(size):
        sh = int(math.log2(size))
        return (row >> sh) == (col >> sh)

    same_chunk = same_block(hc)
    incl = same_chunk & (row >= col)
    strict = same_chunk & (row > col)
    eye = (row == col).astype(F32)
    sizes = [GDN_BASE * 2 ** i for i in range(int(math.log2(hc // GDN_BASE)) + 1)]
    base_mask = same_block(GDN_BASE) & (row > col)
    level_masks = [same_block(big) & jnp.logical_not(same_block(small))
                   for small, big in zip(sizes[:-1], sizes[1:])]
    top_rows = row < hc
    s_scr[...] = jnp.zeros_like(s_scr)
    row8 = lax.broadcasted_iota(I32, (8, B_W), 0)

    def split(a):
        hi = a.astype(BF16)
        return hi, (a - hi.astype(F32)).astype(BF16)

    def mm3(a, b):
        ah, al = a
        bh, bl = b
        r = _dot(ah, bh)
        if bl is not None:
            r = r + _dot(ah, bl)
        if al is not None:
            r = r + _dot(al, bh)
        return r

    def unit_lower_inverse(ms, fillers):
        fillers = list(fillers)

        def fill():
            if fillers:
                fillers.pop(0)()
        n1 = [(jnp.where(base_mask, -m, 0.0).astype(BF16), None) for m in ms]
        d = [eye + x[0].astype(F32) for x in n1]
        pw = n1
        for _ in range(int(math.log2(GDN_BASE)) - 1):
            pw = [split(mm3(x, x)) for x in pw]
            fill()
            d = [x + mm3(split(x), p) for x, p in zip(d, pw)]
        for lm in level_masks:
            ds = [split(x) for x in d]
            t = [split(mm3(y, (jnp.where(lm, m, 0.0).astype(BF16), None))) for y, m in zip(ds, ms)]
            fill()
            d = [x - mm3(tt, y) for x, tt, y in zip(d, t, ds)]
        while fillers:
            fill()
        return d

    def conv_silu(ref, n, wofs):
        r0 = pl.multiple_of(n * c, c)
        cur = ref[0, pl.ds(r0, c), :].astype(F32)
        pr0 = pl.multiple_of(jnp.maximum(n, 1) * c - 16, 16)
        tail = ref[0, pl.ds(pr0, 16), :].astype(F32) * jnp.where(n > 0, 1.0, 0.0)
        y = cur * cw_ref[CONV_WIDTH - 1:CONV_WIDTH, wofs:wofs + B_W]
        for s in range(1, CONV_WIDTH):
            sh = pltpu.roll(cur, s, 0)
            top = jnp.where(row8 < s, pltpu.roll(tail, s, 0)[:8], sh[:8])
            sh = jnp.concatenate([top, sh[8:]], axis=0)
            y = y + sh * cw_ref[CONV_WIDTH - 1 - s:CONV_WIDTH - s, wofs:wofs + B_W]
        return _silu(y)

    def prep(n):
        r0 = pl.multiple_of(n * c, c)
        kc = conv_silu(k_ref, n, B_W)
        sm = sm_ref[0, pl.ds(r0, c), :]
        g = -jnp.exp(alog_ref[...]) * _softplus(sm + dtb_ref[...])
        beta = _sigmoid(sm)
        s = 1
        while s < hc:
            g = g + jnp.where((row & (hc - 1)) >= s, pltpu.roll(g, s, 0), 0.0)
            s *= 2
        g_t = g.T
        heads = range(B_HEADS)
        sls = [slice(h * 128, (h + 1) * 128) for h in heads]
        gcol = [jnp.broadcast_to(g[:, SM_BA + h:SM_BA + h + 1], (c, LANES)) for h in heads]
        grow = [jnp.broadcast_to(g_t[SM_BA + h:SM_BA + h + 1, :], (c, LANES)) for h in heads]
        bcol = [jnp.broadcast_to(beta[:, SM_BB + h:SM_BB + h + 1], (c, LANES)) for h in heads]
        kh = [kc[:, sl] for sl in sls]
        kh = [x * lax.rsqrt(jnp.sum(x * x, axis=-1, keepdims=True) + EPS) for x in kh]
        khb = [x.astype(BF16) for x in kh]
        decay = [jnp.exp(jnp.where(incl, gc - gr, -jnp.inf)) for gc, gr in zip(gcol, grow)]
        kbeta = [k * b for k, b in zip(kh, bcol)]
        ms = [jnp.where(strict, _nt_dot(kb.astype(BF16), k) * dc, 0.0) for kb, k, dc in zip(kbeta, khb, decay)]
        egc = [jnp.exp(x) for x in gcol]
        env = {}

        def fill_q():
            qc = conv_silu(q_ref, n, 0)
            qh = [qc[:, sl] for sl in sls]
            env["qh"] = [x * lax.rsqrt(jnp.sum(x * x, axis=-1, keepdims=True) + EPS) * (B_HEAD_DIM ** -0.5)
                         for x in qh]

        def fill_v():
            vc = conv_silu(v_ref, n, 2 * B_W)
            env["rhs"] = [jnp.concatenate([(vc[:, sl] * b).astype(BF16), (kb * e).astype(BF16)], axis=1)
                          for sl, b, kb, e in zip(sls, bcol, kbeta, egc)]

        def fill_qk():
            env["a_intra"] = [(_nt_dot(q.astype(BF16), k) * dc).astype(BF16)
                              for q, k, dc in zip(env["qh"], khb, decay)]
            env["qg"] = [(q * e).astype(BF16) for q, e in zip(env["qh"], egc)]

        def finish(inv):
            a_intra, qg, rhs = env["a_intra"], env["qg"], env["rhs"]
            uw = [_dot(i.astype(BF16), r) for i, r in zip(inv, rhs)]
            u = [x[:, :128] for x in uw]
            wb = [x[:, 128:].astype(BF16) for x in uw]
            kd_t = [(k * jnp.exp(jnp.where(top_rows, gc[hc - 1:hc, :], gc[c - 1:c, :]) - gc)).T
                    for k, gc in zip(kh, gcol)]
            st = [s_scr[h] for h in heads]
            o_parts = [[] for _ in heads]
            for ci in range(c // hc):
                rs = slice(ci * hc, (ci + 1) * hc)
                in_chunk = (col >> int(math.log2(hc))) == ci
                stb = [x.astype(BF16) for x in st]
                v_new = [u[h][rs] - _dot(wb[h][rs], stb[h]) for h in heads]
                vn_pad = [jnp.concatenate([x] * (c // hc), axis=0).astype(BF16) for x in v_new]
                for h in heads:
                    o_parts[h].append(_dot(qg[h][rs], stb[h]) + _dot(a_intra[h][rs], vn_pad[h]))
                st = [st[h] * jnp.exp(gcol[h][(ci + 1) * hc - 1:(ci + 1) * hc, :])
                      + _dot(jnp.where(in_chunk, kd_t[h], 0.0).astype(BF16), vn_pad[h]) for h in heads]
            for h in heads:
                s_scr[h] = st[h]
                o = jnp.concatenate(o_parts[h], axis=0)
                zo = z_ref[0, pl.ds(r0, c), sls[h]].astype(F32)
                on = o * lax.rsqrt(jnp.mean(o * o, axis=-1, keepdims=True) + EPS) * gn_ref[...]
                y_ref[0, pl.ds(r0, c), sls[h]] = (on * _silu(zo)).astype(y_ref.dtype)

        return ms, [fill_q, fill_v, fill_qk], finish

    def step(i, carry):
        parts = [prep(GDN_UNROLL * i + u) for u in range(GDN_UNROLL)]
        ms = [m for p in parts for m in p[0]]
        fillers = [f for group in zip(*[p[1] for p in parts]) for f in group]
        inv = unit_lower_inverse(ms, fillers)
        for u, p in enumerate(parts):
            p[2](inv[u * B_HEADS:(u + 1) * B_HEADS])
        return carry

    assert nc % GDN_UNROLL == 0
    lax.fori_loop(0, nc // GDN_UNROLL, step, 0)


def _gdn_call(p3, sm3, conv_w, alog_v, dtb_v, gnorm):
    bsz, seq, _ = p3.shape
    col = lambda g: (lambda b: (b, 0, g))
    vec = lambda b: (0, 0)
    return pl.pallas_call(
        _gdn_body,
        grid=(bsz,),
        in_specs=[
            pl.BlockSpec((1, seq, B_W), col(3)), pl.BlockSpec((1, seq, B_W), col(4)),
            pl.BlockSpec((1, seq, B_W), col(5)), pl.BlockSpec((1, seq, B_W), col(6)),
            pl.BlockSpec((1, seq, LANES), lambda b: (b, 0, 0)),
            pl.BlockSpec((CONV_WIDTH, 3 * B_W), vec),
            pl.BlockSpec((1, LANES), vec), pl.BlockSpec((1, LANES), vec), pl.BlockSpec((1, LANES), vec),
        ],
        out_specs=pl.BlockSpec((1, seq, B_W), lambda b: (b, 0, 0)),
        out_shape=jax.ShapeDtypeStruct((bsz, seq, B_W), BF16),
        scratch_shapes=[pltpu.VMEM((B_HEADS, 128, 128), F32)],
        compiler_params=pltpu.CompilerParams(
            dimension_semantics=("arbitrary",), vmem_limit_bytes=VMEM_LIMIT),
        name="gdn",
    )(p3, p3, p3, p3, sm3, conv_w, alog_v, dtb_v, gnorm)


def _ret_body(qk_ref, v_ref, g_ref, cos_ref, sin_ref, dmat_ref, xi_ref, zeta_ref, gch_ref,
              y_ref, s_scr):
    c = RET_BLK
    nc = v_ref.shape[1] // c
    lane = lax.broadcasted_iota(I32, (c, LANES), 1)
    first = (lane & 32) == 0
    lo_head = lane < 64
    s_scr[...] = jnp.zeros_like(s_scr)

    def chunk(n, carry):
        r0 = pl.multiple_of(n * c, c)
        qk = qk_ref[0, pl.ds(r0, c), :].astype(F32)
        cs = cos_ref[pl.ds(r0, c), :]
        sn = sin_ref[pl.ds(r0, c), :]
        for p in range(C_HEADS // 2):
            qp = _rot_half32(qk[:, p * 128:(p + 1) * 128], cs, sn, first)
            kp = _rot_half32(qk[:, C_QK + p * 128:C_QK + (p + 1) * 128], cs, sn, first) * (C_KEY_DIM ** -0.5)
            qm = [jnp.where(lo_head, qp, 0.0).astype(BF16), jnp.where(lo_head, 0.0, qp).astype(BF16)]
            km = [jnp.where(lo_head, kp, 0.0), jnp.where(lo_head, 0.0, kp)]
            sc = _nt_dot(jnp.concatenate(qm, axis=0), kp.astype(BF16))
            for e in range(2):
                h = 2 * p + e
                sl = slice(h * 128, (h + 1) * 128)
                vh = v_ref[0, pl.ds(r0, c), sl].astype(F32)
                scores = sc[e * c:(e + 1) * c] * dmat_ref[h]
                st = s_scr[h]
                o = _dot(scores.astype(BF16), vh.astype(BF16)) + _dot(qm[e], st.astype(BF16)) * xi_ref[h]
                s_scr[h] = st * gch_ref[h, 0:1, :] + _dot(km[e].T.astype(BF16), (vh * zeta_ref[h]).astype(BF16))
                on = o * lax.rsqrt(jnp.mean(o * o, axis=-1, keepdims=True) + EPS)
                go = g_ref[0, pl.ds(r0, c), sl].astype(F32)
                y_ref[0, pl.ds(r0, c), sl] = (on * _silu(go)).astype(y_ref.dtype)
        return carry

    def step(i, carry):
        for u in range(RET_UNROLL):
            carry = chunk(RET_UNROLL * i + u, carry)
        return carry
    assert nc % RET_UNROLL == 0
    lax.fori_loop(0, nc // RET_UNROLL, step, 0)


def _ret_call(p3, cos_t, sin_t, dmat, xi, zeta, gch):
    bsz, seq, _ = p3.shape
    c = RET_BLK
    z2 = lambda b: (0, 0)
    z3 = lambda b: (0, 0, 0)
    return pl.pallas_call(
        _ret_body,
        grid=(bsz,),
        in_specs=[
            pl.BlockSpec((1, seq, 512), lambda b: (b, 0, 7)),
            pl.BlockSpec((1, seq, 512), lambda b: (b, 0, 8)),
            pl.BlockSpec((1, seq, 512), lambda b: (b, 0, 9)),
            pl.BlockSpec((seq, LANES), z2), pl.BlockSpec((seq, LANES), z2),
            pl.BlockSpec((C_HEADS, c, c), z3),
            pl.BlockSpec((C_HEADS, c, LANES), z3), pl.BlockSpec((C_HEADS, c, LANES), z3),
            pl.BlockSpec((C_HEADS, 8, LANES), z3),
        ],
        out_specs=pl.BlockSpec((1, seq, C_W), lambda b: (b, 0, 0)),
        out_shape=jax.ShapeDtypeStruct((bsz, seq, C_W), BF16),
        scratch_shapes=[pltpu.VMEM((C_HEADS, 128, 128), F32)],
        compiler_params=pltpu.CompilerParams(
            dimension_semantics=("arbitrary",), vmem_limit_bytes=VMEM_LIMIT),
        name="ret",
    )(p3, p3, p3, cos_t, sin_t, dmat, xi, zeta, gch)


def _merge_body(ya_ref, yb_ref, yc_ref, g0_ref, g1_ref, g2_ref, x_ref, wb_ref, wo_ref, ln_ref, o_ref):
    def gate(ref):
        return _sigmoid(ref[...].astype(F32))
    merged = (gate(g0_ref) * _dot(ya_ref[...], wb_ref[0:A_W, :])
              + gate(g1_ref) * _dot(yb_ref[...], wb_ref[A_W:A_W + B_W, :])
              + gate(g2_ref) * _dot(yc_ref[...], wb_ref[A_W + B_W:, :]))
    o = _dot(merged.astype(BF16), wo_ref[...])
    on = o * lax.rsqrt(jnp.mean(o * o, axis=-1, keepdims=True) + EPS) * ln_ref[...]
    o_ref[...] = x_ref[...] + on


def _merge_call(ya, yb, yc, p2d, x2d, wb, wo, ln):
    n = x2d.shape[0]
    tm = min(TM_DENSE, n)
    rowb = lambda i: (i, 0)
    cst = lambda i: (0, 0)
    gcol = lambda g: (lambda i: (i, g))
    return pl.pallas_call(
        _merge_body,
        grid=(n // tm,),
        in_specs=[
            pl.BlockSpec((tm, A_W), rowb), pl.BlockSpec((tm, B_W), rowb), pl.BlockSpec((tm, C_W), rowb),
            pl.BlockSpec((tm, D_MODEL), gcol(5)), pl.BlockSpec((tm, D_MODEL), gcol(6)),
            pl.BlockSpec((tm, D_MODEL), gcol(7)),
            pl.BlockSpec((tm, D_MODEL), rowb),
            pl.BlockSpec((A_W + B_W + C_W, D_MODEL), cst, pipeline_mode=pl.Buffered(1)),
            pl.BlockSpec((D_MODEL, D_MODEL), cst, pipeline_mode=pl.Buffered(1)),
            pl.BlockSpec((1, D_MODEL), cst),
        ],
        out_specs=pl.BlockSpec((tm, D_MODEL), rowb),
        out_shape=jax.ShapeDtypeStruct((n, D_MODEL), F32),
        compiler_params=pltpu.CompilerParams(
            dimension_semantics=("arbitrary",), vmem_limit_bytes=VMEM_LIMIT),
        name="merge",
    )(ya, yb, yc, p2d, p2d, p2d, x2d, wb, wo, ln)


def _mlp_body(x_ref, lnpre_ref, wu_ref, wd_ref, lnpost_ref, o_ref):
    x = x_ref[...]
    h = (x * lax.rsqrt(jnp.mean(x * x, axis=-1, keepdims=True) + EPS) * lnpre_ref[...]).astype(BF16)
    ff = jnp.zeros(x.shape, F32)
    for cidx in range(D_FF // D_MODEL):
        sl = slice(cidx * D_MODEL, (cidx + 1) * D_MODEL)
        u = jnp.maximum(_dot(h, wu_ref[:, sl]), 0.0)
        ff = ff + _dot((u * u).astype(BF16), wd_ref[sl, :])
    o_ref[...] = x + ff * lax.rsqrt(jnp.mean(ff * ff, axis=-1, keepdims=True) + EPS) * lnpost_ref[...]


def _mlp_call(x2d, lnpre, wu, wd, lnpost):
    n = x2d.shape[0]
    tm = min(TM_DENSE, n)
    rowb = lambda i: (i, 0)
    cst = lambda i: (0, 0)
    return pl.pallas_call(
        _mlp_body,
        grid=(n // tm,),
        in_specs=[
            pl.BlockSpec((tm, D_MODEL), rowb),
            pl.BlockSpec((1, D_MODEL), cst),
            pl.BlockSpec((D_MODEL, D_FF), cst, pipeline_mode=pl.Buffered(1)),
            pl.BlockSpec((D_FF, D_MODEL), cst, pipeline_mode=pl.Buffered(1)),
            pl.BlockSpec((1, D_MODEL), cst),
        ],
        out_specs=pl.BlockSpec((tm, D_MODEL), rowb),
        out_shape=jax.ShapeDtypeStruct((n, D_MODEL), F32),
        compiler_params=pltpu.CompilerParams(
            dimension_semantics=("arbitrary",), vmem_limit_bytes=VMEM_LIMIT),
        name="mlp",
    )(x2d, lnpre, wu, wd, lnpost)


def _rope_tables(seq):
    pos = jnp.arange(seq, dtype=F32)
    inv_a = ROPE_THETA ** (-jnp.arange(0, A_HEAD_DIM, 2, dtype=F32) / A_HEAD_DIM)
    ang_a = pos[:, None] * inv_a[None, :]
    cosa = jnp.concatenate([jnp.cos(ang_a)] * 2, axis=1)
    sina = jnp.concatenate([-jnp.sin(ang_a), jnp.sin(ang_a)], axis=1)
    inv_i = ROPE_THETA ** (-jnp.arange(0, IDX_DIM, 2, dtype=F32) / IDX_DIM)
    ang_i = pos[:, None] * inv_i[None, :]
    cosi = jnp.concatenate([jnp.cos(ang_i)] * 4, axis=1)
    sini = jnp.concatenate([-jnp.sin(ang_i), jnp.sin(ang_i)] * 2, axis=1)
    inv_c = 1.0 / (ROPE_THETA ** jnp.linspace(0.0, 1.0, C_KEY_DIM // 2, dtype=F32))
    ang_c = pos[:, None] * inv_c[None, :]
    cosc = jnp.concatenate([jnp.cos(ang_c)] * 4, axis=1)
    sinc = jnp.concatenate([-jnp.sin(ang_c), jnp.sin(ang_c)] * 2, axis=1)
    return (cosa, sina, cosi, sini), (cosc, sinc)


def _ret_tables():
    c = RET_BLK
    log_gamma = jnp.log(1.0 - 2.0 ** (-5.0 - jnp.arange(C_HEADS, dtype=F32)))
    r = jnp.arange(c, dtype=F32)
    rel = r[:, None] - r[None, :]
    dmat = jnp.where(rel >= 0, jnp.exp(jnp.maximum(rel, 0.0)[None] * log_gamma[:, None, None]), 0.0)
    xi = jnp.exp((r + 1.0)[None] * log_gamma[:, None])
    zeta = jnp.exp((c - 1.0 - r)[None] * log_gamma[:, None])
    gch = jnp.exp(c * log_gamma)
    xi_b = jnp.broadcast_to(xi[:, :, None], (C_HEADS, c, LANES))
    zeta_b = jnp.broadcast_to(zeta[:, :, None], (C_HEADS, c, LANES))
    gch_b = jnp.broadcast_to(gch[:, None, None], (C_HEADS, 8, LANES))
    return dmat, xi_b, zeta_b, gch_b


def _lane_vec(vals, ofs):
    v = jnp.zeros((1, LANES), F32)
    return v.at[0, ofs:ofs + vals.shape[0]].set(vals.astype(F32))


def kernel(x, ln_mix_pre, w_in, gdn_conv, gdn_a_log, gdn_dt_bias, gdn_norm, w_branch, w_out,
           ln_mix_post, ln_mlp_pre, w_up, w_down, ln_mlp_post):
    bsz, seq, _ = x.shape
    depth = w_in.shape[0]
    n = bsz * seq
    n_sel = min(TOPK_MAX, seq // 4)
    assert seq % BLK == 0 and n_sel % BLK == 0
    assert seq % (2 * KBLK) == 0
    dsa_tabs, (cosc, sinc) = _rope_tables(seq)
    dmat, xi_b, zeta_b, gch_b = _ret_tables()

    x2d = x.reshape(n, D_MODEL)
    for l in range(depth):
        w_big, w_small = _prep_w_in(w_in[l])
        p2d, sm2d = _proj_call(x2d, ln_mix_pre[l][None, :], w_big, w_small)
        p3 = p2d.reshape(bsz, seq, P_WIDTH)
        sm3 = sm2d.reshape(bsz, seq, LANES)
        y_a = _dsa_call(p3, sm3, dsa_tabs, n_sel)
        y_b = _gdn_call(p3, sm3, gdn_conv[l], _lane_vec(gdn_a_log[l], SM_BA),
                        _lane_vec(gdn_dt_bias[l], SM_BA), gdn_norm[l][None, :])
        y_c = _ret_call(p3, cosc, sinc, dmat, xi_b, zeta_b, gch_b)
        x2d = _merge_call(y_a.reshape(n, A_W), y_b.reshape(n, B_W), y_c.reshape(n, C_W), p2d, x2d,
                          w_branch[l].astype(BF16), w_out[l].astype(BF16), ln_mix_post[l][None, :])
        x2d = _mlp_call(x2d, ln_mlp_pre[l][None, :], w_up[l].astype(BF16), w_down[l].astype(BF16),
                        ln_mlp_post[l][None, :])
    return x2d.reshape(bsz, seq, D_MODEL)
```

```python
import functools
import math

import numpy as np
import jax
import jax.numpy as jnp
from jax import lax
from jax.experimental import pallas as pl
from jax.experimental.pallas import tpu as pltpu

F32 = jnp.float32
BF16 = jnp.bfloat16
I32 = jnp.int32

D_MODEL = 1024
A_HEADS = 4
A_HEAD_DIM = 128
IDX_HEADS = 8
IDX_DIM = 64
TOPK_MAX = 256
B_HEADS = 4
B_HEAD_DIM = 128
CONV_WIDTH = 4
C_HEADS = 4
C_KEY_DIM = 64
C_VAL_DIM = 128
D_FF = 4 * D_MODEL
ROPE_THETA = 10000.0
EPS = 1e-6
N_BRANCH = 3
A_W = A_HEADS * A_HEAD_DIM
B_W = B_HEADS * B_HEAD_DIM
C_W = C_HEADS * C_VAL_DIM
C_QK = C_HEADS * C_KEY_DIM

LANES = 128
BLK = 128
KBLK = 256
GDN_BLK = 128
GDN_CHUNK = 64
GDN_BASE = 8
GDN_UNROLL = 4
RET_BLK = 128
RET_UNROLL = 4
TM_DENSE = 1024
VMEM_LIMIT = 52 * 1024 * 1024

_OFF = {}
_o = 0
for _name, _w in (("a_q", A_W), ("a_k", A_HEAD_DIM), ("a_v", A_HEAD_DIM),
                  ("i_q", IDX_HEADS * IDX_DIM), ("i_k", IDX_DIM), ("i_w", IDX_HEADS),
                  ("b_q", B_W), ("b_k", B_W), ("b_v", B_W), ("b_a", B_HEADS), ("b_b", B_HEADS),
                  ("b_z", B_W), ("c_q", C_QK), ("c_k", C_QK), ("c_v", C_W), ("c_g", C_W),
                  ("gate", N_BRANCH * D_MODEL)):
    _OFF[_name] = (_o, _w)
    _o += _w

P_WIDTH = 8192
P_TILE = 1024
SM_IW = 0
SM_BA = 8
SM_BB = 12

KEY_NEG_INF = -2139095041
INT_MIN = -2147483648


def _cols(w, name):
    o, n = _OFF[name]
    return w[:, o:o + n]


def _prep_w_in(w):
    ik = _cols(w, "i_k")
    big = jnp.concatenate([
        _cols(w, "a_q"), _cols(w, "a_k"), _cols(w, "a_v"), ik, ik, ik, ik,
        _cols(w, "i_q"),
        _cols(w, "b_q"), _cols(w, "b_k"), _cols(w, "b_v"), _cols(w, "b_z"),
        _cols(w, "c_q"), _cols(w, "c_k"), _cols(w, "c_v"), _cols(w, "c_g"),
        _cols(w, "gate")], axis=1)
    assert big.shape[1] == P_WIDTH
    small = jnp.concatenate([
        _cols(w, "i_w"), _cols(w, "b_a"), _cols(w, "b_b"),
        jnp.zeros((w.shape[0], LANES - 16), w.dtype)], axis=1)
    return big.astype(BF16), small.astype(BF16)


def _nt_dot(a, b):
    return lax.dot_general(a, b, (((1,), (1,)), ((), ())), preferred_element_type=F32)


def _dot(a, b):
    return jnp.dot(a, b, preferred_element_type=F32)


def _sigmoid(x):
    return 0.5 + 0.5 * jnp.tanh(0.5 * x)


def _silu(x):
    return x * _sigmoid(x)


def _softplus(x):
    return jnp.maximum(x, 0.0) + jnp.log(1.0 + jnp.exp(-jnp.abs(x)))


def _rot_half64(x, cos, sin_signed):
    return x * cos + pltpu.roll(x, 64, 1) * sin_signed


def _rot_half32(x, cos, sin_signed, first):
    partner = jnp.where(first, pltpu.roll(x, 96, 1), pltpu.roll(x, 32, 1))
    return x * cos + partner * sin_signed


def _proj_body(x_ref, gain_ref, w_ref, ws_ref, p_ref, s_ref):
    x = x_ref[...]
    ms = jnp.mean(x * x, axis=-1, keepdims=True)
    h = (x * lax.rsqrt(ms + EPS) * gain_ref[...]).astype(BF16)
    s_ref[...] = _dot(h, ws_ref[...])
    for t in range(P_WIDTH // P_TILE):
        sl = slice(t * P_TILE, (t + 1) * P_TILE)
        p_ref[:, sl] = _dot(h, w_ref[:, sl]).astype(p_ref.dtype)


def _proj_call(x2d, gain, w_big, w_small):
    n = x2d.shape[0]
    tm = min(512, n)
    once = pl.Buffered(1)
    return pl.pallas_call(
        _proj_body,
        grid=(n // tm,),
        in_specs=[
            pl.BlockSpec((tm, D_MODEL), lambda i: (i, 0)),
            pl.BlockSpec((1, D_MODEL), lambda i: (0, 0)),
            pl.BlockSpec((D_MODEL, P_WIDTH), lambda i: (0, 0), pipeline_mode=once),
            pl.BlockSpec((D_MODEL, LANES), lambda i: (0, 0), pipeline_mode=once),
        ],
        out_specs=[
            pl.BlockSpec((tm, P_WIDTH), lambda i: (i, 0)),
            pl.BlockSpec((tm, LANES), lambda i: (i, 0)),
        ],
        out_shape=[
            jax.ShapeDtypeStruct((n, P_WIDTH), BF16),
            jax.ShapeDtypeStruct((n, LANES), F32),
        ],
        compiler_params=pltpu.CompilerParams(
            dimension_semantics=("arbitrary",), vmem_limit_bytes=VMEM_LIMIT),
        name="proj",
    )(x2d, gain, w_big, w_small)


def _dsa_body(n_sel, aq_ref, kv_ref, iq_ref, sm_ref,
              cosa_ref, sina_ref, cosi_ref, sini_ref,
              cosaq_ref, sinaq_ref, cosiq_ref, siniq_ref, tri_ref,
              y_ref, ka_scr, ki_scr, vt_scr, key_scr, rank_scr, s_scr, acc_scr):
    j = pl.program_id(1)
    nkb = ka_scr.shape[0]
    kblk = ka_scr.shape[1]
    seq = nkb * kblk
    nk = (j * BLK + BLK + kblk - 1) // kblk
    lane = lax.broadcasted_iota(I32, (BLK, LANES), 1)
    klane = lax.broadcasted_iota(I32, (kblk, LANES), 1)
    krow = lax.broadcasted_iota(I32, (kblk, LANES), 0)
    first = (lane & 32) == 0
    kfirst = (klane & 32) == 0
    lo_head = lane < 64

    @pl.when(j == 0)
    def _():
        def prep(kb, c):
            r0 = pl.multiple_of(kb * kblk, kblk)
            kv = kv_ref[0, pl.ds(r0, kblk), :].astype(F32)
            ca = cosa_ref[pl.ds(r0, kblk), :]
            sa = sina_ref[pl.ds(r0, kblk), :]
            ci = cosi_ref[pl.ds(r0, kblk), :]
            si = sini_ref[pl.ds(r0, kblk), :]
            ka_scr[kb] = _rot_half64(kv[:, 0:128], ca, sa).astype(BF16)
            ki_scr[kb] = _rot_half32(kv[:, 256:384], ci, si, kfirst).astype(BF16)
            vt_scr[kb] = kv[:, 128:256].T.astype(BF16)
            return c
        lax.fori_loop(0, nkb, prep, 0)

    aq = aq_ref[0].astype(F32)
    caq = cosaq_ref[...]
    saq = sinaq_ref[...]
    qa = jnp.concatenate(
        [(_rot_half64(aq[:, h * 128:(h + 1) * 128], caq, saq) * (A_HEAD_DIM ** -0.5)).astype(BF16)
         for h in range(A_HEADS)], axis=0)

    iq = iq_ref[0].astype(F32)
    ciq = cosiq_ref[...]
    siq = siniq_ref[...]
    rows = []
    for p in range(IDX_HEADS // 2):
        rp = _rot_half32(iq[:, p * 128:(p + 1) * 128], ciq, siq, first)
        rows.append(jnp.where(lo_head, rp, 0.0).astype(BF16))
        rows.append(jnp.where(lo_head, 0.0, rp).astype(BF16))
    qm = jnp.concatenate(rows, axis=0)

    w_t = sm_ref[0].T * ((IDX_DIM ** -0.5) * (IDX_HEADS ** -0.5))

    def score_blk(kb, c):
        lg = _nt_dot(ki_scr[kb], qm)
        s_scr[kb] = _nt_dot(ka_scr[kb], qa)
        sc = jnp.zeros((kblk, LANES), F32)
        for h in range(IDX_HEADS):
            sc = sc + jnp.maximum(lg[:, h * BLK:(h + 1) * BLK], 0.0) * w_t[SM_IW + h:SM_IW + h + 1, :]
        sc = jnp.where(sc == 0.0, 0.0, sc)
        bits = pltpu.bitcast(sc, I32)
        key = bits ^ ((bits >> 31) & 0x7FFFFFFF)
        causal = (kb * kblk + krow) <= (j * BLK + klane)
        key_scr[kb] = jnp.where(causal, key, KEY_NEG_INF)
        return c
    nk2 = (nk + 1) // 2

    def for_block_pairs(body, init):
        return lax.fori_loop(0, nk2, lambda kp, c: body(2 * kp + 1, body(2 * kp, c)), init)

    for_block_pairs(score_blk, 0)

    def count(pred):
        def body(kp, acc):
            for kb in (2 * kp, 2 * kp + 1):
                m = pred(key_scr[kb], kb * kblk + krow).astype(I32)
                acc = acc + m.reshape(kblk // 8, 8, LANES).sum(axis=0)
            return acc
        acc = lax.fori_loop(0, nk2, body, jnp.zeros((8, LANES), I32))
        return acc.sum(axis=0, keepdims=True)

    def search():
        def bit_step(i, tu):
            cand = tu | lax.shift_left(jnp.int32(1), 31 - i)
            cs = cand ^ INT_MIN
            cnt = count(lambda k, idx: k >= cs)
            return jnp.where(cnt >= n_sel, cand, tu)
        tu = lax.fori_loop(0, 32, bit_step, jnp.zeros((1, LANES), I32))
        return tu ^ INT_MIN

    def take_all():
        return jnp.full((1, LANES), KEY_NEG_INF, I32)

    t = lax.cond((j + 1) * BLK > n_sel, search, take_all)

    def rank_blk(kb, carry):
        off, cgt = carry
        key = key_scr[kb]
        tie = jnp.where(key == t, 1.0, 0.0).astype(BF16)
        pre = _dot(tri_ref[...], tie)
        rank_scr[kb] = pre + off
        cgt = cgt + (key > t).astype(I32).reshape(kblk // 8, 8, LANES).sum(axis=0)
        return off + pre[kblk - 1:kblk, :], cgt
    _, cgt = for_block_pairs(rank_blk, (jnp.zeros((1, LANES), F32), jnp.zeros((8, LANES), I32)))
    need = jnp.where(t == KEY_NEG_INF, 0, n_sel - cgt.sum(axis=0, keepdims=True)).astype(F32)

    def mask_blk(kb, macc):
        key = key_scr[kb]
        sel = (key > t) | ((key == t) & (rank_scr[kb] <= need))
        s = s_scr[kb]
        sm = jnp.concatenate(
            [jnp.where(sel, s[:, h * BLK:(h + 1) * BLK], -jnp.inf) for h in range(A_HEADS)], axis=1)
        s_scr[kb] = sm
        return jnp.maximum(macc, sm.reshape(kblk // 8, 8, A_HEADS * BLK).max(axis=0))
    macc = for_block_pairs(mask_blk, jnp.full((8, A_HEADS * BLK), -jnp.inf, F32))
    m = macc.max(axis=0, keepdims=True)

    acc_scr[...] = jnp.zeros_like(acc_scr)

    def pv_blk(kb, lacc):
        p = jnp.exp(s_scr[kb] - m)
        acc_scr[...] += _dot(vt_scr[kb], p.astype(BF16))
        return lacc + p.reshape(kblk // 8, 8, A_HEADS * BLK).sum(axis=0)
    lacc = for_block_pairs(pv_blk, jnp.zeros((8, A_HEADS * BLK), F32))
    inv_l = 1.0 / lacc.sum(axis=0, keepdims=True)
    o_t = acc_scr[...] * inv_l
    for h in range(A_HEADS):
        y_ref[0, :, h * 128:(h + 1) * 128] = o_t[:, h * BLK:(h + 1) * BLK].T.astype(y_ref.dtype)


def _dsa_call(p3, sm3, tabs, n_sel):
    bsz, seq, _ = p3.shape
    nqb = seq // BLK
    kblk = min(KBLK, seq)
    nkb = seq // kblk
    cosa, sina, cosi, sini = tabs
    full = lambda b, j: (0, 0)
    qblk = lambda b, j: (j, 0)
    return pl.pallas_call(
        functools.partial(_dsa_body, n_sel),
        grid=(bsz, nqb),
        in_specs=[
            pl.BlockSpec((1, BLK, 512), lambda b, j: (b, j, 0)),
            pl.BlockSpec((1, seq, 512), lambda b, j: (b, 0, 1)),
            pl.BlockSpec((1, BLK, 512), lambda b, j: (b, j, 2)),
            pl.BlockSpec((1, BLK, LANES), lambda b, j: (b, j, 0)),
            pl.BlockSpec((seq, LANES), full), pl.BlockSpec((seq, LANES), full),
            pl.BlockSpec((seq, LANES), full), pl.BlockSpec((seq, LANES), full),
            pl.BlockSpec((BLK, LANES), qblk), pl.BlockSpec((BLK, LANES), qblk),
            pl.BlockSpec((BLK, LANES), qblk), pl.BlockSpec((BLK, LANES), qblk),
            pl.BlockSpec((kblk, kblk), full),
        ],
        out_specs=pl.BlockSpec((1, BLK, A_W), lambda b, j: (b, j, 0)),
        out_shape=jax.ShapeDtypeStruct((bsz, seq, A_W), BF16),
        scratch_shapes=[
            pltpu.VMEM((nkb, kblk, 128), BF16),
            pltpu.VMEM((nkb, kblk, 128), BF16),
            pltpu.VMEM((nkb, 128, kblk), BF16),
            pltpu.VMEM((nkb, kblk, LANES), I32),
            pltpu.VMEM((nkb, kblk, LANES), F32),
            pltpu.VMEM((nkb, kblk, A_HEADS * BLK), F32),
            pltpu.VMEM((128, A_HEADS * BLK), F32),
        ],
        compiler_params=pltpu.CompilerParams(
            dimension_semantics=("arbitrary", "arbitrary"), vmem_limit_bytes=VMEM_LIMIT),
        name="dsa",
    )(p3, p3, p3, sm3, cosa, sina, cosi, sini, cosa, sina, cosi, sini, jnp.tril(jnp.ones((kblk, kblk), BF16)))


def _gdn_body(q_ref, k_ref, v_ref, z_ref, sm_ref, cw_ref, alog_ref, dtb_ref, gn_ref,
              y_ref, s_scr):
    c = GDN_BLK
    hc = GDN_CHUNK
    nc = q_ref.shape[1] // c
    row = lax.broadcasted_iota(I32, (c, LANES), 0)
    col = lax.broadcasted_iota(I32, (c, LANES), 1)

    def same_block(size):
        sh = int(math.log2(size))
        return (row >> sh) == (col >> sh)

    same_chunk = same_block(hc)
    incl = same_chunk & (row >= col)
    strict = same_chunk & (row > col)
    eye = (row == col).astype(F32)
    sizes = [GDN_BASE * 2 ** i for i in range(int(math.log2(hc // GDN_BASE)) + 1)]
    base_mask = same_block(GDN_BASE) & (row > col)
    level_masks = [same_block(big) & jnp.logical_not(same_block(small))
                   for small, big in zip(sizes[:-1], sizes[1:])]
    top_rows = row < hc
    s_scr[...] = jnp.zeros_like(s_scr)
    row8 = lax.broadcasted_iota(I32, (8, B_W), 0)

    def split(a):
        hi = a.astype(BF16)
        return hi, (a - hi.astype(F32)).astype(BF16)

    def mm3(a, b):
        ah, al = a
        bh, bl = b
        r = _dot(ah, bh)
        if bl is not None:
            r = r + _dot(ah, bl)
        if al is not None:
            r = r + _dot(al, bh)
        return r

    def unit_lower_inverse(ms, fillers):
        fillers = list(fillers)

        def fill():
            if fillers:
                fillers.pop(0)()
        n1 = [(jnp.where(base_mask, -m, 0.0).astype(BF16), None) for m in ms]
        d = [eye + x[0].astype(F32) for x in n1]
        pw = n1
        for _ in range(int(math.log2(GDN_BASE)) - 1):
            pw = [split(mm3(x, x)) for x in pw]
            fill()
            d = [x + mm3(split(x), p) for x, p in zip(d, pw)]
        for lm in level_masks:
            ds = [split(x) for x in d]
            t = [split(mm3(y, (jnp.where(lm, m, 0.0).astype(BF16), None))) for y, m in zip(ds, ms)]
            fill()
            d = [x - mm3(tt, y) for x, tt, y in zip(d, t, ds)]
        while fillers:
            fill()
        return d

    def conv_silu(ref, n, wofs):
        r0 = pl.multiple_of(n * c, c)
        cur = ref[0, pl.ds(r0, c), :].astype(F32)
        pr0 = pl.multiple_of(jnp.maximum(n, 1) * c - 16, 16)
        tail = ref[0, pl.ds(pr0, 16), :].astype(F32) * jnp.where(n > 0, 1.0, 0.0)
        y = cur * cw_ref[CONV_WIDTH - 1:CONV_WIDTH, wofs:wofs + B_W]
        for s in range(1, CONV_WIDTH):
            sh = pltpu.roll(cur, s, 0)
            top = jnp.where(row8 < s, pltpu.roll(tail, s, 0)[:8], sh[:8])
            sh = jnp.concatenate([top, sh[8:]], axis=0)
            y = y + sh * cw_ref[CONV_WIDTH - 1 - s:CONV_WIDTH - s, wofs:wofs + B_W]
        return _silu(y)

    def prep(n):
        r0 = pl.multiple_of(n * c, c)
        kc = conv_silu(k_ref, n, B_W)
        sm = sm_ref[0, pl.ds(r0, c), :]
        g = -jnp.exp(alog_ref[...]) * _softplus(sm + dtb_ref[...])
        beta = _sigmoid(sm)
        s = 1
        while s < hc:
            g = g + jnp.where((row & (hc - 1)) >= s, pltpu.roll(g, s, 0), 0.0)
            s *= 2
        g_t = g.T
        heads = range(B_HEADS)
        sls = [slice(h * 128, (h + 1) * 128) for h in heads]
        gcol = [jnp.broadcast_to(g[:, SM_BA + h:SM_BA + h + 1], (c, LANES)) for h in heads]
        grow = [jnp.broadcast_to(g_t[SM_BA + h:SM_BA + h + 1, :], (c, LANES)) for h in heads]
        bcol = [jnp.broadcast_to(beta[:, SM_BB + h:SM_BB + h + 1], (c, LANES)) for h in heads]
        kh = [kc[:, sl] for sl in sls]
        kh = [x * lax.rsqrt(jnp.sum(x * x, axis=-1, keepdims=True) + EPS) for x in kh]
        khb = [x.astype(BF16) for x in kh]
        decay = [jnp.exp(jnp.where(incl, gc - gr, -jnp.inf)) for gc, gr in zip(gcol, grow)]
        kbeta = [k * b for k, b in zip(kh, bcol)]
        ms = [jnp.where(strict, _nt_dot(kb.astype(BF16), k) * dc, 0.0) for kb, k, dc in zip(kbeta, khb, decay)]
        egc = [jnp.exp(x) for x in gcol]
        env = {}

        def fill_q():
            qc = conv_silu(q_ref, n, 0)
            qh = [qc[:, sl] for sl in sls]
            env["qh"] = [x * lax.rsqrt(jnp.sum(x * x, axis=-1, keepdims=True) + EPS) * (B_HEAD_DIM ** -0.5)
                         for x in qh]

        def fill_v():
            vc = conv_silu(v_ref, n, 2 * B_W)
            env["rhs"] = [jnp.concatenate([(vc[:, sl] * b).astype(BF16), (kb * e).astype(BF16)], axis=1)
                          for sl, b, kb, e in zip(sls, bcol, kbeta, egc)]

        def fill_qk():
            env["a_intra"] = [(_nt_dot(q.astype(BF16), k) * dc).astype(BF16)
                              for q, k, dc in zip(env["qh"], khb, decay)]
            env["qg"] = [(q * e).astype(BF16) for q, e in zip(env["qh"], egc)]

        def finish(inv):
            a_intra, qg, rhs = env["a_intra"], env["qg"], env["rhs"]
            uw = [_dot(i.astype(BF16), r) for i, r in zip(inv, rhs)]
            u = [x[:, :128] for x in uw]
            wb = [x[:, 128:].astype(BF16) for x in uw]
            kd_t = [(k * jnp.exp(jnp.where(top_rows, gc[hc - 1:hc, :], gc[c - 1:c, :]) - gc)).T
                    for k, gc in zip(kh, gcol)]
            st = [s_scr[h] for h in heads]
            o_parts = [[] for _ in heads]
            for ci in range(c // hc):
                rs = slice(ci * hc, (ci + 1) * hc)
                in_chunk = (col >> int(math.log2(hc))) == ci
                stb = [x.astype(BF16) for x in st]
                v_new = [u[h][rs] - _dot(wb[h][rs], stb[h]) for h in heads]
                vn_pad = [jnp.concatenate([x] * (c // hc), axis=0).astype(BF16) for x in v_new]
                for h in heads:
                    o_parts[h].append(_dot(qg[h][rs], stb[h]) + _dot(a_intra[h][rs], vn_pad[h]))
                st = [st[h] * jnp.exp(gcol[h][(ci + 1) * hc - 1:(ci + 1) * hc, :])
                      + _dot(jnp.where(in_chunk, kd_t[h], 0.0).astype(BF16), vn_pad[h]) for h in heads]
            for h in heads:
                s_scr[h] = st[h]
                o = jnp.concatenate(o_parts[h], axis=0)
                zo = z_ref[0, pl.ds(r0, c), sls[h]].astype(F32)
                on = o * lax.rsqrt(jnp.mean(o * o, axis=-1, keepdims=True) + EPS) * gn_ref[...]
                y_ref[0, pl.ds(r0, c), sls[h]] = (on * _silu(zo)).astype(y_ref.dtype)

        return ms, [fill_q, fill_v, fill_qk], finish

    def step(i, carry):
        parts = [prep(GDN_UNROLL * i + u) for u in range(GDN_UNROLL)]
        ms = [m for p in parts for m in p[0]]
        fillers = [f for group in zip(*[p[1] for p in parts]) for f in group]
        inv = unit_lower_inverse(ms, fillers)
        for u, p in enumerate(parts):
            p[2](inv[u * B_HEADS:(u + 1) * B_HEADS])
        return carry

    assert nc % GDN_UNROLL == 0
    lax.fori_loop(0, nc // GDN_UNROLL, step, 0)


def _gdn_call(p3, sm3, conv_w, alog_v, dtb_v, gnorm):
    bsz, seq, _ = p3.shape
    col = lambda g: (lambda b: (b, 0, g))
    vec = lambda b: (0, 0)
    return pl.pallas_call(
        _gdn_body,
        grid=(bsz,),
        in_specs=[
            pl.BlockSpec((1, seq, B_W), col(3)), pl.BlockSpec((1, seq, B_W), col(4)),
            pl.BlockSpec((1, seq, B_W), col(5)), pl.BlockSpec((1, seq, B_W), col(6)),
            pl.BlockSpec((1, seq, LANES), lambda b: (b, 0, 0)),
            pl.BlockSpec((CONV_WIDTH, 3 * B_W), vec),
            pl.BlockSpec((1, LANES), vec), pl.BlockSpec((1, LANES), vec), pl.BlockSpec((1, LANES), vec),
        ],
        out_specs=pl.BlockSpec((1, seq, B_W), lambda b: (b, 0, 0)),
        out_shape=jax.ShapeDtypeStruct((bsz, seq, B_W), BF16),
        scratch_shapes=[pltpu.VMEM((B_HEADS, 128, 128), F32)],
        compiler_params=pltpu.CompilerParams(
            dimension_semantics=("arbitrary",), vmem_limit_bytes=VMEM_LIMIT),
        name="gdn",
    )(p3, p3, p3, p3, sm3, conv_w, alog_v, dtb_v, gnorm)


def _ret_body(qk_ref, v_ref, g_ref, cos_ref, sin_ref, dmat_ref, xi_ref, zeta_ref, gch_ref,
              y_ref, s_scr):
    c = RET_BLK
    nc = v_ref.shape[1] // c
    lane = lax.broadcasted_iota(I32, (c, LANES), 1)
    first = (lane & 32) == 0
    lo_head = lane < 64
    s_scr[...] = jnp.zeros_like(s_scr)

    def chunk(n, carry):
        r0 = pl.multiple_of(n * c, c)
        qk = qk_ref[0, pl.ds(r0, c), :].astype(F32)
        cs = cos_ref[pl.ds(r0, c), :]
        sn = sin_ref[pl.ds(r0, c), :]
        for p in range(C_HEADS // 2):
            qp = _rot_half32(qk[:, p * 128:(p + 1) * 128], cs, sn, first)
            kp = _rot_half32(qk[:, C_QK + p * 128:C_QK + (p + 1) * 128], cs, sn, first) * (C_KEY_DIM ** -0.5)
            qm = [jnp.where(lo_head, qp, 0.0).astype(BF16), jnp.where(lo_head, 0.0, qp).astype(BF16)]
            km = [jnp.where(lo_head, kp, 0.0), jnp.where(lo_head, 0.0, kp)]
            sc = _nt_dot(jnp.concatenate(qm, axis=0), kp.astype(BF16))
            for e in range(2):
                h = 2 * p + e
                sl = slice(h * 128, (h + 1) * 128)
                vh = v_ref[0, pl.ds(r0, c), sl].astype(F32)
                scores = sc[e * c:(e + 1) * c] * dmat_ref[h]
                st = s_scr[h]
                o = _dot(scores.astype(BF16), vh.astype(BF16)) + _dot(qm[e], st.astype(BF16)) * xi_ref[h]
                s_scr[h] = st * gch_ref[h, 0:1, :] + _dot(km[e].T.astype(BF16), (vh * zeta_ref[h]).astype(BF16))
                on = o * lax.rsqrt(jnp.mean(o * o, axis=-1, keepdims=True) + EPS)
                go = g_ref[0, pl.ds(r0, c), sl].astype(F32)
                y_ref[0, pl.ds(r0, c), sl] = (on * _silu(go)).astype(y_ref.dtype)
        return carry

    def step(i, carry):
        for u in range(RET_UNROLL):
            carry = chunk(RET_UNROLL * i + u, carry)
        return carry
    assert nc % RET_UNROLL == 0
    lax.fori_loop(0, nc // RET_UNROLL, step, 0)


def _ret_call(p3, cos_t, sin_t, dmat, xi, zeta, gch):
    bsz, seq, _ = p3.shape
    c = RET_BLK
    z2 = lambda b: (0, 0)
    z3 = lambda b: (0, 0, 0)
    return pl.pallas_call(
        _ret_body,
        grid=(bsz,),
        in_specs=[
            pl.BlockSpec((1, seq, 512), lambda b: (b, 0, 7)),
            pl.BlockSpec((1, seq, 512), lambda b: (b, 0, 8)),
            pl.BlockSpec((1, seq, 512), lambda b: (b, 0, 9)),
            pl.BlockSpec((seq, LANES), z2), pl.BlockSpec((seq, LANES), z2),
            pl.BlockSpec((C_HEADS, c, c), z3),
            pl.BlockSpec((C_HEADS, c, LANES), z3), pl.BlockSpec((C_HEADS, c, LANES), z3),
            pl.BlockSpec((C_HEADS, 8, LANES), z3),
        ],
        out_specs=pl.BlockSpec((1, seq, C_W), lambda b: (b, 0, 0)),
        out_shape=jax.ShapeDtypeStruct((bsz, seq, C_W), BF16),
        scratch_shapes=[pltpu.VMEM((C_HEADS, 128, 128), F32)],
        compiler_params=pltpu.CompilerParams(
            dimension_semantics=("arbitrary",), vmem_limit_bytes=VMEM_LIMIT),
        name="ret",
    )(p3, p3, p3, cos_t, sin_t, dmat, xi, zeta, gch)


def _merge_body(ya_ref, yb_ref, yc_ref, g0_ref, g1_ref, g2_ref, x_ref, wb_ref, wo_ref, ln_ref, o_ref):
    def gate(ref):
        return _sigmoid(ref[...].astype(F32))
    merged = (gate(g0_ref) * _dot(ya_ref[...], wb_ref[0:A_W, :])
              + gate(g1_ref) * _dot(yb_ref[...], wb_ref[A_W:A_W + B_W, :])
              + gate(g2_ref) * _dot(yc_ref[...], wb_ref[A_W + B_W:, :]))
    o = _dot(merged.astype(BF16), wo_ref[...])
    on = o * lax.rsqrt(jnp.mean(o * o, axis=-1, keepdims=True) + EPS) * ln_ref[...]
    o_ref[...] = x_ref[...] + on


def _merge_call(ya, yb, yc, p2d, x2d, wb, wo, ln):
    n = x2d.shape[0]
    tm = min(TM_DENSE, n)
    rowb = lambda i: (i, 0)
    cst = lambda i: (0, 0)
    gcol = lambda g: (lambda i: (i, g))
    return pl.pallas_call(
        _merge_body,
        grid=(n // tm,),
        in_specs=[
            pl.BlockSpec((tm, A_W), rowb), pl.BlockSpec((tm, B_W), rowb), pl.BlockSpec((tm, C_W), rowb),
            pl.BlockSpec((tm, D_MODEL), gcol(5)), pl.BlockSpec((tm, D_MODEL), gcol(6)),
            pl.BlockSpec((tm, D_MODEL), gcol(7)),
            pl.BlockSpec((tm, D_MODEL), rowb),
            pl.BlockSpec((A_W + B_W + C_W, D_MODEL), cst, pipeline_mode=pl.Buffered(1)),
            pl.BlockSpec((D_MODEL, D_MODEL), cst, pipeline_mode=pl.Buffered(1)),
            pl.BlockSpec((1, D_MODEL), cst),
        ],
        out_specs=pl.BlockSpec((tm, D_MODEL), rowb),
        out_shape=jax.ShapeDtypeStruct((n, D_MODEL), F32),
        compiler_params=pltpu.CompilerParams(
            dimension_semantics=("arbitrary",), vmem_limit_bytes=VMEM_LIMIT),
        name="merge",
    )(ya, yb, yc, p2d, p2d, p2d, x2d, wb, wo, ln)


def _mlp_body(x_ref, lnpre_ref, wu_ref, wd_ref, lnpost_ref, o_ref):
    x = x_ref[...]
    h = (x * lax.rsqrt(jnp.mean(x * x, axis=-1, keepdims=True) + EPS) * lnpre_ref[...]).astype(BF16)
    ff = jnp.zeros(x.shape, F32)
    for cidx in range(D_FF // D_MODEL):
        sl = slice(cidx * D_MODEL, (cidx + 1) * D_MODEL)
        u = jnp.maximum(_dot(h, wu_ref[:, sl]), 0.0)
        ff = ff + _dot((u * u).astype(BF16), wd_ref[sl, :])
    o_ref[...] = x + ff * lax.rsqrt(jnp.mean(ff * ff, axis=-1, keepdims=True) + EPS) * lnpost_ref[...]


def _mlp_call(x2d, lnpre, wu, wd, lnpost):
    n = x2d.shape[0]
    tm = min(TM_DENSE, n)
    rowb = lambda i: (i, 0)
    cst = lambda i: (0, 0)
    return pl.pallas_call(
        _mlp_body,
        grid=(n // tm,),
        in_specs=[
            pl.BlockSpec((tm, D_MODEL), rowb),
            pl.BlockSpec((1, D_MODEL), cst),
            pl.BlockSpec((D_MODEL, D_FF), cst, pipeline_mode=pl.Buffered(1)),
            pl.BlockSpec((D_FF, D_MODEL), cst, pipeline_mode=pl.Buffered(1)),
            pl.BlockSpec((1, D_MODEL), cst),
        ],
        out_specs=pl.BlockSpec((tm, D_MODEL), rowb),
        out_shape=jax.ShapeDtypeStruct((n, D_MODEL), F32),
        compiler_params=pltpu.CompilerParams(
            dimension_semantics=("arbitrary",), vmem_limit_bytes=VMEM_LIMIT),
        name="mlp",
    )(x2d, lnpre, wu, wd, lnpost)


TM_CHANNEL = 512


def _channel_body(ya_ref, yb_ref, yc_ref, g0_ref, g1_ref, g2_ref, x_ref, wb_ref, wo_ref, lnmix_ref,
                  lnpre_ref, wu_ref, wd_ref, lnpost_ref, o_ref):
    def gate(ref):
        return _sigmoid(ref[...].astype(F32))
    merged = (gate(g0_ref) * _dot(ya_ref[...], wb_ref[0:A_W, :])
              + gate(g1_ref) * _dot(yb_ref[...], wb_ref[A_W:A_W + B_W, :])
              + gate(g2_ref) * _dot(yc_ref[...], wb_ref[A_W + B_W:, :]))
    o = _dot(merged.astype(BF16), wo_ref[...])
    x = x_ref[...] + o * lax.rsqrt(jnp.mean(o * o, axis=-1, keepdims=True) + EPS) * lnmix_ref[...]
    h = (x * lax.rsqrt(jnp.mean(x * x, axis=-1, keepdims=True) + EPS) * lnpre_ref[...]).astype(BF16)
    ff = jnp.zeros(x.shape, F32)
    for cidx in range(D_FF // D_MODEL):
        sl = slice(cidx * D_MODEL, (cidx + 1) * D_MODEL)
        u = jnp.maximum(_dot(h, wu_ref[:, sl]), 0.0)
        ff = ff + _dot((u * u).astype(BF16), wd_ref[sl, :])
    o_ref[...] = x + ff * lax.rsqrt(jnp.mean(ff * ff, axis=-1, keepdims=True) + EPS) * lnpost_ref[...]


def _channel_call(ya, yb, yc, p2d, x2d, wb, wo, lnmix, lnpre, wu, wd, lnpost):
    n = x2d.shape[0]
    tm = min(TM_CHANNEL, n)
    rowb = lambda i: (i, 0)
    cst = lambda i: (0, 0)
    gcol = lambda g: (lambda i: (i, g))
    once = pl.Buffered(1)
    vec = pl.BlockSpec((1, D_MODEL), cst)
    return pl.pallas_call(
        _channel_body,
        grid=(n // tm,),
        in_specs=[
            pl.BlockSpec((tm, A_W), rowb), pl.BlockSpec((tm, B_W), rowb), pl.BlockSpec((tm, C_W), rowb),
            pl.BlockSpec((tm, D_MODEL), gcol(5)), pl.BlockSpec((tm, D_MODEL), gcol(6)),
            pl.BlockSpec((tm, D_MODEL), gcol(7)),
            pl.BlockSpec((tm, D_MODEL), rowb),
            pl.BlockSpec((A_W + B_W + C_W, D_MODEL), cst, pipeline_mode=once),
            pl.BlockSpec((D_MODEL, D_MODEL), cst, pipeline_mode=once),
            vec, vec,
            pl.BlockSpec((D_MODEL, D_FF), cst, pipeline_mode=once),
            pl.BlockSpec((D_FF, D_MODEL), cst, pipeline_mode=once),
            vec,
        ],
        out_specs=pl.BlockSpec((tm, D_MODEL), rowb),
        out_shape=jax.ShapeDtypeStruct((n, D_MODEL), F32),
        compiler_params=pltpu.CompilerParams(
            dimension_semantics=("arbitrary",), vmem_limit_bytes=VMEM_LIMIT),
        name="channel",
    )(ya, yb, yc, p2d, p2d, p2d, x2d, wb, wo, lnmix, lnpre, wu, wd, lnpost)


def _rope_tables(seq):
    pos = jnp.arange(seq, dtype=F32)
    inv_a = ROPE_THETA ** (-jnp.arange(0, A_HEAD_DIM, 2, dtype=F32) / A_HEAD_DIM)
    ang_a = pos[:, None] * inv_a[None, :]
    cosa = jnp.concatenate([jnp.cos(ang_a)] * 2, axis=1)
    sina = jnp.concatenate([-jnp.sin(ang_a), jnp.sin(ang_a)], axis=1)
    inv_i = ROPE_THETA ** (-jnp.arange(0, IDX_DIM, 2, dtype=F32) / IDX_DIM)
    ang_i = pos[:, None] * inv_i[None, :]
    cosi = jnp.concatenate([jnp.cos(ang_i)] * 4, axis=1)
    sini = jnp.concatenate([-jnp.sin(ang_i), jnp.sin(ang_i)] * 2, axis=1)
    inv_c = 1.0 / (ROPE_THETA ** jnp.linspace(0.0, 1.0, C_KEY_DIM // 2, dtype=F32))
    ang_c = pos[:, None] * inv_c[None, :]
    cosc = jnp.concatenate([jnp.cos(ang_c)] * 4, axis=1)
    sinc = jnp.concatenate([-jnp.sin(ang_c), jnp.sin(ang_c)] * 2, axis=1)
    return (cosa, sina, cosi, sini), (cosc, sinc)


def _ret_tables():
    c = RET_BLK
    log_gamma = jnp.log(1.0 - 2.0 ** (-5.0 - jnp.arange(C_HEADS, dtype=F32)))
    r = jnp.arange(c, dtype=F32)
    rel = r[:, None] - r[None, :]
    dmat = jnp.where(rel >= 0, jnp.exp(jnp.maximum(rel, 0.0)[None] * log_gamma[:, None, None]), 0.0)
    xi = jnp.exp((r + 1.0)[None] * log_gamma[:, None])
    zeta = jnp.exp((c - 1.0 - r)[None] * log_gamma[:, None])
    gch = jnp.exp(c * log_gamma)
    xi_b = jnp.broadcast_to(xi[:, :, None], (C_HEADS, c, LANES))
    zeta_b = jnp.broadcast_to(zeta[:, :, None], (C_HEADS, c, LANES))
    gch_b = jnp.broadcast_to(gch[:, None, None], (C_HEADS, 8, LANES))
    return dmat, xi_b, zeta_b, gch_b


def _lane_vec(vals, ofs):
    v = jnp.zeros((1, LANES), F32)
    return v.at[0, ofs:ofs + vals.shape[0]].set(vals.astype(F32))


def kernel(x, ln_mix_pre, w_in, gdn_conv, gdn_a_log, gdn_dt_bias, gdn_norm, w_branch, w_out,
           ln_mix_post, ln_mlp_pre, w_up, w_down, ln_mlp_post):
    bsz, seq, _ = x.shape
    depth = w_in.shape[0]
    n = bsz * seq
    n_sel = min(TOPK_MAX, seq // 4)
    assert seq % BLK == 0 and n_sel % BLK == 0
    assert seq % (2 * KBLK) == 0
    dsa_tabs, (cosc, sinc) = _rope_tables(seq)
    dmat, xi_b, zeta_b, gch_b = _ret_tables()

    x2d = x.reshape(n, D_MODEL)
    for l in range(depth):
        w_big, w_small = _prep_w_in(w_in[l])
        p2d, sm2d = _proj_call(x2d, ln_mix_pre[l][None, :], w_big, w_small)
        p3 = p2d.reshape(bsz, seq, P_WIDTH)
        sm3 = sm2d.reshape(bsz, seq, LANES)
        y_a = _dsa_call(p3, sm3, dsa_tabs, n_sel)
        y_b = _gdn_call(p3, sm3, gdn_conv[l], _lane_vec(gdn_a_log[l], SM_BA),
                        _lane_vec(gdn_dt_bias[l], SM_BA), gdn_norm[l][None, :])
        y_c = _ret_call(p3, cosc, sinc, dmat, xi_b, zeta_b, gch_b)
        x2d = _channel_call(y_a.reshape(n, A_W), y_b.reshape(n, B_W), y_c.reshape(n, C_W), p2d, x2d,
                            w_branch[l].astype(BF16), w_out[l].astype(BF16), ln_mix_post[l][None, :],
                            ln_mlp_pre[l][None, :], w_up[l].astype(BF16), w_down[l].astype(BF16),
                            ln_mlp_post[l][None, :])
    return x2d.reshape(bsz, seq, D_MODEL)
```

```python
import functools
import math

import numpy as np
import jax
import jax.numpy as jnp
from jax import lax
from jax.experimental import pallas as pl
from jax.experimental.pallas import tpu as pltpu

F32 = jnp.float32
BF16 = jnp.bfloat16
I32 = jnp.int32

D_MODEL = 1024
A_HEADS = 4
A_HEAD_DIM = 128
IDX_HEADS = 8
IDX_DIM = 64
TOPK_MAX = 256
B_HEADS = 4
B_HEAD_DIM = 128
CONV_WIDTH = 4
C_HEADS = 4
C_KEY_DIM = 64
C_VAL_DIM = 128
D_FF = 4 * D_MODEL
ROPE_THETA = 10000.0
EPS = 1e-6
N_BRANCH = 3
A_W = A_HEADS * A_HEAD_DIM
B_W = B_HEADS * B_HEAD_DIM
C_W = C_HEADS * C_VAL_DIM
C_QK = C_HEADS * C_KEY_DIM

LANES = 128
BLK = 128
KBLK = 256
GDN_BLK = 128
GDN_CHUNK = 64
GDN_BASE = 8
GDN_UNROLL = 4
RET_BLK = 128
RET_UNROLL = 4
TM_DENSE = 1024
VMEM_LIMIT = 52 * 1024 * 1024

_OFF = {}
_o = 0
for _name, _w in (("a_q", A_W), ("a_k", A_HEAD_DIM), ("a_v", A_HEAD_DIM),
                  ("i_q", IDX_HEADS * IDX_DIM), ("i_k", IDX_DIM), ("i_w", IDX_HEADS),
                  ("b_q", B_W), ("b_k", B_W), ("b_v", B_W), ("b_a", B_HEADS), ("b_b", B_HEADS),
                  ("b_z", B_W), ("c_q", C_QK), ("c_k", C_QK), ("c_v", C_W), ("c_g", C_W),
                  ("gate", N_BRANCH * D_MODEL)):
    _OFF[_name] = (_o, _w)
    _o += _w

P_WIDTH = 8192
P_TILE = 1024
SM_IW = 0
SM_BA = 8
SM_BB = 12

KEY_NEG_INF = -2139095041
INT_MIN = -2147483648


def _cols(w, name):
    o, n = _OFF[name]
    return w[:, o:o + n]


def _prep_w_in(w):
    ik = _cols(w, "i_k")
    big = jnp.concatenate([
        _cols(w, "a_q"), _cols(w, "a_k"), _cols(w, "a_v"), ik, ik, ik, ik,
        _cols(w, "i_q"),
        _cols(w, "b_q"), _cols(w, "b_k"), _cols(w, "b_v"), _cols(w, "b_z"),
        _cols(w, "c_q"), _cols(w, "c_k"), _cols(w, "c_v"), _cols(w, "c_g"),
        _cols(w, "gate")], axis=1)
    assert big.shape[1] == P_WIDTH
    small = jnp.concatenate([
        _cols(w, "i_w"), _cols(w, "b_a"), _cols(w, "b_b"),
        jnp.zeros((w.shape[0], LANES - 16), w.dtype)], axis=1)
    return big.astype(BF16), small.astype(BF16)


def _nt_dot(a, b):
    return lax.dot_general(a, b, (((1,), (1,)), ((), ())), preferred_element_type=F32)


def _dot(a, b):
    return jnp.dot(a, b, preferred_element_type=F32)


def _sigmoid(x):
    return 0.5 + 0.5 * jnp.tanh(0.5 * x)


def _silu(x):
    return x * _sigmoid(x)


def _softplus(x):
    return jnp.maximum(x, 0.0) + jnp.log(1.0 + jnp.exp(-jnp.abs(x)))


def _rot_half64(x, cos, sin_signed):
    return x * cos + pltpu.roll(x, 64, 1) * sin_signed


def _rot_half32(x, cos, sin_signed, first):
    partner = jnp.where(first, pltpu.roll(x, 96, 1), pltpu.roll(x, 32, 1))
    return x * cos + partner * sin_signed


def _proj_body(x_ref, gain_ref, w_ref, ws_ref, p_ref, s_ref):
    x = x_ref[...]
    ms = jnp.mean(x * x, axis=-1, keepdims=True)
    h = (x * lax.rsqrt(ms + EPS) * gain_ref[...]).astype(BF16)
    s_ref[...] = _dot(h, ws_ref[...])
    for t in range(P_WIDTH // P_TILE):
        sl = slice(t * P_TILE, (t + 1) * P_TILE)
        p_ref[:, sl] = _dot(h, w_ref[:, sl]).astype(p_ref.dtype)


def _proj_call(x2d, gain, w_big, w_small):
    n = x2d.shape[0]
    tm = min(512, n)
    once = pl.Buffered(1)
    return pl.pallas_call(
        _proj_body,
        grid=(n // tm,),
        in_specs=[
            pl.BlockSpec((tm, D_MODEL), lambda i: (i, 0)),
            pl.BlockSpec((1, D_MODEL), lambda i: (0, 0)),
            pl.BlockSpec((D_MODEL, P_WIDTH), lambda i: (0, 0), pipeline_mode=once),
            pl.BlockSpec((D_MODEL, LANES), lambda i: (0, 0), pipeline_mode=once),
        ],
        out_specs=[
            pl.BlockSpec((tm, P_WIDTH), lambda i: (i, 0)),
            pl.BlockSpec((tm, LANES), lambda i: (i, 0)),
        ],
        out_shape=[
            jax.ShapeDtypeStruct((n, P_WIDTH), BF16),
            jax.ShapeDtypeStruct((n, LANES), F32),
        ],
        compiler_params=pltpu.CompilerParams(
            dimension_semantics=("arbitrary",), vmem_limit_bytes=VMEM_LIMIT),
        name="proj",
    )(x2d, gain, w_big, w_small)


def _dsa_body(n_sel, aq_ref, kv_ref, iq_ref, sm_ref,
              cosa_ref, sina_ref, cosi_ref, sini_ref,
              cosaq_ref, sinaq_ref, cosiq_ref, siniq_ref, tri_ref,
              y_ref, ka_scr, ki_scr, vt_scr, key_scr, s_scr, acc_scr):
    j = pl.program_id(1)
    nkb = ka_scr.shape[0]
    kblk = ka_scr.shape[1]
    seq = nkb * kblk
    nk = (j * BLK + BLK + kblk - 1) // kblk
    lane = lax.broadcasted_iota(I32, (BLK, LANES), 1)
    klane = lax.broadcasted_iota(I32, (kblk, LANES), 1)
    krow = lax.broadcasted_iota(I32, (kblk, LANES), 0)
    first = (lane & 32) == 0
    kfirst = (klane & 32) == 0
    lo_head = lane < 64

    @pl.when(j == 0)
    def _():
        def prep(kb, c):
            r0 = pl.multiple_of(kb * kblk, kblk)
            kv = kv_ref[0, pl.ds(r0, kblk), :].astype(F32)
            ca = cosa_ref[pl.ds(r0, kblk), :]
            sa = sina_ref[pl.ds(r0, kblk), :]
            ci = cosi_ref[pl.ds(r0, kblk), :]
            si = sini_ref[pl.ds(r0, kblk), :]
            ka_scr[kb] = _rot_half64(kv[:, 0:128], ca, sa).astype(BF16)
            ki_scr[kb] = _rot_half32(kv[:, 256:384], ci, si, kfirst).astype(BF16)
            vt_scr[kb] = kv[:, 128:256].T.astype(BF16)
            return c
        lax.fori_loop(0, nkb, prep, 0)

    aq = aq_ref[0].astype(F32)
    caq = cosaq_ref[...]
    saq = sinaq_ref[...]
    qa = jnp.concatenate(
        [(_rot_half64(aq[:, h * 128:(h + 1) * 128], caq, saq) * (A_HEAD_DIM ** -0.5)).astype(BF16)
         for h in range(A_HEADS)], axis=0)

    iq = iq_ref[0].astype(F32)
    ciq = cosiq_ref[...]
    siq = siniq_ref[...]
    rows = []
    for p in range(IDX_HEADS // 2):
        rp = _rot_half32(iq[:, p * 128:(p + 1) * 128], ciq, siq, first)
        rows.append(jnp.where(lo_head, rp, 0.0).astype(BF16))
        rows.append(jnp.where(lo_head, 0.0, rp).astype(BF16))
    qm = jnp.concatenate(rows, axis=0)

    w_t = sm_ref[0].T * ((IDX_DIM ** -0.5) * (IDX_HEADS ** -0.5))

    def score_blk(kb, c):
        lg = _nt_dot(ki_scr[kb], qm)
        s_scr[kb] = _nt_dot(ka_scr[kb], qa)
        sc = jnp.zeros((kblk, LANES), F32)
        for h in range(IDX_HEADS):
            sc = sc + jnp.maximum(lg[:, h * BLK:(h + 1) * BLK], 0.0) * w_t[SM_IW + h:SM_IW + h + 1, :]
        sc = jnp.where(sc == 0.0, 0.0, sc)
        bits = pltpu.bitcast(sc, I32)
        key = bits ^ ((bits >> 31) & 0x7FFFFFFF)
        causal = (kb * kblk + krow) <= (j * BLK + klane)
        key_scr[kb] = jnp.where(causal, key, KEY_NEG_INF)
        return c
    nk2 = (nk + 1) // 2

    def for_block_pairs(body, init):
        return lax.fori_loop(0, nk2, lambda kp, c: body(2 * kp + 1, body(2 * kp, c)), init)

    for_block_pairs(score_blk, 0)

    def count(pred):
        def body(kp, acc):
            for kb in (2 * kp, 2 * kp + 1):
                m = pred(key_scr[kb], kb * kblk + krow).astype(I32)
                acc = acc + m.reshape(kblk // 8, 8, LANES).sum(axis=0)
            return acc
        acc = lax.fori_loop(0, nk2, body, jnp.zeros((8, LANES), I32))
        return acc.sum(axis=0, keepdims=True)

    def search():
        def bit_step(i, tu):
            cand = tu | lax.shift_left(jnp.int32(1), 31 - i)
            cs = cand ^ INT_MIN
            cnt = count(lambda k, idx: k >= cs)
            return jnp.where(cnt >= n_sel, cand, tu)
        tu = lax.fori_loop(0, 32, bit_step, jnp.zeros((1, LANES), I32))
        t = tu ^ INT_MIN
        return t, n_sel - count(lambda k, idx: k > t)

    def take_all():
        return jnp.full((1, LANES), KEY_NEG_INF, I32), jnp.zeros((1, LANES), I32)

    t, need = lax.cond((j + 1) * BLK > n_sel, search, take_all)
    need = need.astype(F32)

    def mask_blk(kb, carry):
        off, macc = carry
        key = key_scr[kb]
        tie = key == t
        pre = _dot(tri_ref[...], jnp.where(tie, 1.0, 0.0).astype(BF16))
        sel = (key > t) | (tie & (pre + off <= need))
        s = s_scr[kb]
        sm = jnp.concatenate(
            [jnp.where(sel, s[:, h * BLK:(h + 1) * BLK], -jnp.inf) for h in range(A_HEADS)], axis=1)
        s_scr[kb] = sm
        return off + pre[kblk - 1:kblk, :], jnp.maximum(macc, sm.reshape(kblk // 8, 8, A_HEADS * BLK).max(axis=0))
    _, macc = for_block_pairs(mask_blk, (jnp.zeros((1, LANES), F32), jnp.full((8, A_HEADS * BLK), -jnp.inf, F32)))
    m = macc.max(axis=0, keepdims=True)

    acc_scr[...] = jnp.zeros_like(acc_scr)

    def pv_blk(kb, lacc):
        p = jnp.exp(s_scr[kb] - m)
        acc_scr[...] += _dot(vt_scr[kb], p.astype(BF16))
        return lacc + p.reshape(kblk // 8, 8, A_HEADS * BLK).sum(axis=0)
    lacc = for_block_pairs(pv_blk, jnp.zeros((8, A_HEADS * BLK), F32))
    inv_l = 1.0 / lacc.sum(axis=0, keepdims=True)
    o_t = acc_scr[...] * inv_l
    for h in range(A_HEADS):
        y_ref[0, :, h * 128:(h + 1) * 128] = o_t[:, h * BLK:(h + 1) * BLK].T.astype(y_ref.dtype)


def _dsa_call(p3, sm3, tabs, n_sel):
    bsz, seq, _ = p3.shape
    nqb = seq // BLK
    kblk = min(KBLK, seq)
    nkb = seq // kblk
    cosa, sina, cosi, sini = tabs
    full = lambda b, j: (0, 0)
    qblk = lambda b, j: (j, 0)
    return pl.pallas_call(
        functools.partial(_dsa_body, n_sel),
        grid=(bsz, nqb),
        in_specs=[
            pl.BlockSpec((1, BLK, 512), lambda b, j: (b, j, 0)),
            pl.BlockSpec((1, seq, 512), lambda b, j: (b, 0, 1)),
            pl.BlockSpec((1, BLK, 512), lambda b, j: (b, j, 2)),
            pl.BlockSpec((1, BLK, LANES), lambda b, j: (b, j, 0)),
            pl.BlockSpec((seq, LANES), full), pl.BlockSpec((seq, LANES), full),
            pl.BlockSpec((seq, LANES), full), pl.BlockSpec((seq, LANES), full),
            pl.BlockSpec((BLK, LANES), qblk), pl.BlockSpec((BLK, LANES), qblk),
            pl.BlockSpec((BLK, LANES), qblk), pl.BlockSpec((BLK, LANES), qblk),
            pl.BlockSpec((kblk, kblk), full),
        ],
        out_specs=pl.BlockSpec((1, BLK, A_W), lambda b, j: (b, j, 0)),
        out_shape=jax.ShapeDtypeStruct((bsz, seq, A_W), BF16),
        scratch_shapes=[
            pltpu.VMEM((nkb, kblk, 128), BF16),
            pltpu.VMEM((nkb, kblk, 128), BF16),
            pltpu.VMEM((nkb, 128, kblk), BF16),
            pltpu.VMEM((nkb, kblk, LANES), I32),
            pltpu.VMEM((nkb, kblk, A_HEADS * BLK), F32),
            pltpu.VMEM((128, A_HEADS * BLK), F32),
        ],
        compiler_params=pltpu.CompilerParams(
            dimension_semantics=("arbitrary", "arbitrary"), vmem_limit_bytes=VMEM_LIMIT),
        name="dsa",
    )(p3, p3, p3, sm3, cosa, sina, cosi, sini, cosa, sina, cosi, sini, jnp.tril(jnp.ones((kblk, kblk), BF16)))


def _gdn_body(q_ref, k_ref, v_ref, z_ref, sm_ref, cw_ref, alog_ref, dtb_ref, gn_ref,
              y_ref, s_scr):
    c = GDN_BLK
    hc = GDN_CHUNK
    nc = q_ref.shape[1] // c
    row = lax.broadcasted_iota(I32, (c, LANES), 0)
    col = lax.broadcasted_iota(I32, (c, LANES), 1)

    def same_block(size):
        sh = int(math.log2(size))
        return (row >> sh) == (col >> sh)

    same_chunk = same_block(hc)
    incl = same_chunk & (row >= col)
    strict = same_chunk & (row > col)
    eye = (row == col).astype(F32)
    sizes = [GDN_BASE * 2 ** i for i in range(int(math.log2(hc // GDN_BASE)) + 1)]
    base_mask = same_block(GDN_BASE) & (row > col)
    level_masks = [same_block(big) & jnp.logical_not(same_block(small))
                   for small, big in zip(sizes[:-1], sizes[1:])]
    top_rows = row < hc
    s_scr[...] = jnp.zeros_like(s_scr)
    row8 = lax.broadcasted_iota(I32, (8, B_W), 0)

    def split(a):
        hi = a.astype(BF16)
        return hi, (a - hi.astype(F32)).astype(BF16)

    def mm3(a, b):
        ah, al = a
        bh, bl = b
        r = _dot(ah, bh)
        if bl is not None:
            r = r + _dot(ah, bl)
        if al is not None:
            r = r + _dot(al, bh)
        return r

    def unit_lower_inverse(ms, fillers):
        fillers = list(fillers)

        def fill():
            if fillers:
                fillers.pop(0)()
        n1 = [(jnp.where(base_mask, -m, 0.0).astype(BF16), None) for m in ms]
        d = [eye + x[0].astype(F32) for x in n1]
        pw = n1
        for _ in range(int(math.log2(GDN_BASE)) - 1):
            pw = [split(mm3(x, x)) for x in pw]
            fill()
            d = [x + mm3(split(x), p) for x, p in zip(d, pw)]
        for lm in level_masks:
            ds = [split(x) for x in d]
            t = [split(mm3(y, (jnp.where(lm, m, 0.0).astype(BF16), None))) for y, m in zip(ds, ms)]
            fill()
            d = [x - mm3(tt, y) for x, tt, y in zip(d, t, ds)]
        while fillers:
            fill()
        return d

    def conv_silu(ref, n, wofs):
        r0 = pl.multiple_of(n * c, c)
        cur = ref[0, pl.ds(r0, c), :].astype(F32)
        pr0 = pl.multiple_of(jnp.maximum(n, 1) * c - 16, 16)
        tail = ref[0, pl.ds(pr0, 16), :].astype(F32) * jnp.where(n > 0, 1.0, 0.0)
        y = cur * cw_ref[CONV_WIDTH - 1:CONV_WIDTH, wofs:wofs + B_W]
        for s in range(1, CONV_WIDTH):
            sh = pltpu.roll(cur, s, 0)
            top = jnp.where(row8 < s, pltpu.roll(tail, s, 0)[:8], sh[:8])
            sh = jnp.concatenate([top, sh[8:]], axis=0)
            y = y + sh * cw_ref[CONV_WIDTH - 1 - s:CONV_WIDTH - s, wofs:wofs + B_W]
        return _silu(y)

    def prep(n):
        r0 = pl.multiple_of(n * c, c)
        kc = conv_silu(k_ref, n, B_W)
        sm = sm_ref[0, pl.ds(r0, c), :]
        g = -jnp.exp(alog_ref[...]) * _softplus(sm + dtb_ref[...])
        beta = _sigmoid(sm)
        s = 1
        while s < hc:
            g = g + jnp.where((row & (hc - 1)) >= s, pltpu.roll(g, s, 0), 0.0)
            s *= 2
        g_t = g.T
        heads = range(B_HEADS)
        sls = [slice(h * 128, (h + 1) * 128) for h in heads]
        gcol = [jnp.broadcast_to(g[:, SM_BA + h:SM_BA + h + 1], (c, LANES)) for h in heads]
        grow = [jnp.broadcast_to(g_t[SM_BA + h:SM_BA + h + 1, :], (c, LANES)) for h in heads]
        bcol = [jnp.broadcast_to(beta[:, SM_BB + h:SM_BB + h + 1], (c, LANES)) for h in heads]
        kh = [kc[:, sl] for sl in sls]
        kh = [x * lax.rsqrt(jnp.sum(x * x, axis=-1, keepdims=True) + EPS) for x in kh]
        khb = [x.astype(BF16) for x in kh]
        decay = [jnp.exp(jnp.where(incl, gc - gr, -jnp.inf)) for gc, gr in zip(gcol, grow)]
        kbeta = [k * b for k, b in zip(kh, bcol)]
        ms = [jnp.where(strict, _nt_dot(kb.astype(BF16), k) * dc, 0.0) for kb, k, dc in zip(kbeta, khb, decay)]
        egc = [jnp.exp(x) for x in gcol]
        env = {}

        def fill_q():
            qc = conv_silu(q_ref, n, 0)
            qh = [qc[:, sl] for sl in sls]
            env["qh"] = [x * lax.rsqrt(jnp.sum(x * x, axis=-1, keepdims=True) + EPS) * (B_HEAD_DIM ** -0.5)
                         for x in qh]

        def fill_v():
            vc = conv_silu(v_ref, n, 2 * B_W)
            env["rhs"] = [jnp.concatenate([(vc[:, sl] * b).astype(BF16), (kb * e).astype(BF16)], axis=1)
                          for sl, b, kb, e in zip(sls, bcol, kbeta, egc)]

        def fill_qk():
            env["a_intra"] = [(_nt_dot(q.astype(BF16), k) * dc).astype(BF16)
                              for q, k, dc in zip(env["qh"], khb, decay)]
            env["qg"] = [(q * e).astype(BF16) for q, e in zip(env["qh"], egc)]

        def finish(inv):
            a_intra, qg, rhs = env["a_intra"], env["qg"], env["rhs"]
            uw = [_dot(i.astype(BF16), r) for i, r in zip(inv, rhs)]
            u = [x[:, :128] for x in uw]
            wb = [x[:, 128:].astype(BF16) for x in uw]
            kd_t = [(k * jnp.exp(jnp.where(top_rows, gc[hc - 1:hc, :], gc[c - 1:c, :]) - gc)).T
                    for k, gc in zip(kh, gcol)]
            st = [s_scr[h] for h in heads]
            o_parts = [[] for _ in heads]
            for ci in range(c // hc):
                rs = slice(ci * hc, (ci + 1) * hc)
                in_chunk = (col >> int(math.log2(hc))) == ci
                stb = [x.astype(BF16) for x in st]
                v_new = [u[h][rs] - _dot(wb[h][rs], stb[h]) for h in heads]
                vn_pad = [jnp.concatenate([x] * (c // hc), axis=0).astype(BF16) for x in v_new]
                for h in heads:
                    o_parts[h].append(_dot(qg[h][rs], stb[h]) + _dot(a_intra[h][rs], vn_pad[h]))
                st = [st[h] * jnp.exp(gcol[h][(ci + 1) * hc - 1:(ci + 1) * hc, :])
                      + _dot(jnp.where(in_chunk, kd_t[h], 0.0).astype(BF16), vn_pad[h]) for h in heads]
            for h in heads:
                s_scr[h] = st[h]
                o = jnp.concatenate(o_parts[h], axis=0)
                zo = z_ref[0, pl.ds(r0, c), sls[h]].astype(F32)
                on = o * lax.rsqrt(jnp.mean(o * o, axis=-1, keepdims=True) + EPS) * gn_ref[...]
                y_ref[0, pl.ds(r0, c), sls[h]] = (on * _silu(zo)).astype(y_ref.dtype)

        return ms, [fill_q, fill_v, fill_qk], finish

    def step(i, carry):
        parts = [prep(GDN_UNROLL * i + u) for u in range(GDN_UNROLL)]
        ms = [m for p in parts for m in p[0]]
        fillers = [f for group in zip(*[p[1] for p in parts]) for f in group]
        inv = unit_lower_inverse(ms, fillers)
        for u, p in enumerate(parts):
            p[2](inv[u * B_HEADS:(u + 1) * B_HEADS])
        return carry

    assert nc % GDN_UNROLL == 0
    lax.fori_loop(0, nc // GDN_UNROLL, step, 0)


def _gdn_call(p3, sm3, conv_w, alog_v, dtb_v, gnorm):
    bsz, seq, _ = p3.shape
    col = lambda g: (lambda b: (b, 0, g))
    vec = lambda b: (0, 0)
    return pl.pallas_call(
        _gdn_body,
        grid=(bsz,),
        in_specs=[
            pl.BlockSpec((1, seq, B_W), col(3)), pl.BlockSpec((1, seq, B_W), col(4)),
            pl.BlockSpec((1, seq, B_W), col(5)), pl.BlockSpec((1, seq, B_W), col(6)),
            pl.BlockSpec((1, seq, LANES), lambda b: (b, 0, 0)),
            pl.BlockSpec((CONV_WIDTH, 3 * B_W), vec),
            pl.BlockSpec((1, LANES), vec), pl.BlockSpec((1, LANES), vec), pl.BlockSpec((1, LANES), vec),
        ],
        out_specs=pl.BlockSpec((1, seq, B_W), lambda b: (b, 0, 0)),
        out_shape=jax.ShapeDtypeStruct((bsz, seq, B_W), BF16),
        scratch_shapes=[pltpu.VMEM((B_HEADS, 128, 128), F32)],
        compiler_params=pltpu.CompilerParams(
            dimension_semantics=("arbitrary",), vmem_limit_bytes=VMEM_LIMIT),
        name="gdn",
    )(p3, p3, p3, p3, sm3, conv_w, alog_v, dtb_v, gnorm)


def _ret_body(qk_ref, v_ref, g_ref, cos_ref, sin_ref, dmat_ref, xi_ref, zeta_ref, gch_ref,
              y_ref, s_scr):
    c = RET_BLK
    nc = v_ref.shape[1] // c
    lane = lax.broadcasted_iota(I32, (c, LANES), 1)
    first = (lane & 32) == 0
    lo_head = lane < 64
    s_scr[...] = jnp.zeros_like(s_scr)

    def chunk(n, carry):
        r0 = pl.multiple_of(n * c, c)
        qk = qk_ref[0, pl.ds(r0, c), :].astype(F32)
        cs = cos_ref[pl.ds(r0, c), :]
        sn = sin_ref[pl.ds(r0, c), :]
        for p in range(C_HEADS // 2):
            qp = _rot_half32(qk[:, p * 128:(p + 1) * 128], cs, sn, first)
            kp = _rot_half32(qk[:, C_QK + p * 128:C_QK + (p + 1) * 128], cs, sn, first) * (C_KEY_DIM ** -0.5)
            qm = [jnp.where(lo_head, qp, 0.0).astype(BF16), jnp.where(lo_head, 0.0, qp).astype(BF16)]
            km = [jnp.where(lo_head, kp, 0.0), jnp.where(lo_head, 0.0, kp)]
            sc = _nt_dot(jnp.concatenate(qm, axis=0), kp.astype(BF16))
            for e in range(2):
                h = 2 * p + e
                sl = slice(h * 128, (h + 1) * 128)
                vh = v_ref[0, pl.ds(r0, c), sl].astype(F32)
                scores = sc[e * c:(e + 1) * c] * dmat_ref[h]
                st = s_scr[h]
                o = _dot(scores.astype(BF16), vh.astype(BF16)) + _dot(qm[e], st.astype(BF16)) * xi_ref[h]
                s_scr[h] = st * gch_ref[h, 0:1, :] + _dot(km[e].T.astype(BF16), (vh * zeta_ref[h]).astype(BF16))
                on = o * lax.rsqrt(jnp.mean(o * o, axis=-1, keepdims=True) + EPS)
                go = g_ref[0, pl.ds(r0, c), sl].astype(F32)
                y_ref[0, pl.ds(r0, c), sl] = (on * _silu(go)).astype(y_ref.dtype)
        return carry

    def step(i, carry):
        for u in range(RET_UNROLL):
            carry = chunk(RET_UNROLL * i + u, carry)
        return carry
    assert nc % RET_UNROLL == 0
    lax.fori_loop(0, nc // RET_UNROLL, step, 0)


def _ret_call(p3, cos_t, sin_t, dmat, xi, zeta, gch):
    bsz, seq, _ = p3.shape
    c = RET_BLK
    z2 = lambda b: (0, 0)
    z3 = lambda b: (0, 0, 0)
    return pl.pallas_call(
        _ret_body,
        grid=(bsz,),
        in_specs=[
            pl.BlockSpec((1, seq, 512), lambda b: (b, 0, 7)),
            pl.BlockSpec((1, seq, 512), lambda b: (b, 0, 8)),
            pl.BlockSpec((1, seq, 512), lambda b: (b, 0, 9)),
            pl.BlockSpec((seq, LANES), z2), pl.BlockSpec((seq, LANES), z2),
            pl.BlockSpec((C_HEADS, c, c), z3),
            pl.BlockSpec((C_HEADS, c, LANES), z3), pl.BlockSpec((C_HEADS, c, LANES), z3),
            pl.BlockSpec((C_HEADS, 8, LANES), z3),
        ],
        out_specs=pl.BlockSpec((1, seq, C_W), lambda b: (b, 0, 0)),
        out_shape=jax.ShapeDtypeStruct((bsz, seq, C_W), BF16),
        scratch_shapes=[pltpu.VMEM((C_HEADS, 128, 128), F32)],
        compiler_params=pltpu.CompilerParams(
            dimension_semantics=("arbitrary",), vmem_limit_bytes=VMEM_LIMIT),
        name="ret",
    )(p3, p3, p3, cos_t, sin_t, dmat, xi, zeta, gch)


def _merge_body(ya_ref, yb_ref, yc_ref, g0_ref, g1_ref, g2_ref, x_ref, wb_ref, wo_ref, ln_ref, o_ref):
    def gate(ref):
        return _sigmoid(ref[...].astype(F32))
    merged = (gate(g0_ref) * _dot(ya_ref[...], wb_ref[0:A_W, :])
              + gate(g1_ref) * _dot(yb_ref[...], wb_ref[A_W:A_W + B_W, :])
              + gate(g2_ref) * _dot(yc_ref[...], wb_ref[A_W + B_W:, :]))
    o = _dot(merged.astype(BF16), wo_ref[...])
    on = o * lax.rsqrt(jnp.mean(o * o, axis=-1, keepdims=True) + EPS) * ln_ref[...]
    o_ref[...] = x_ref[...] + on


def _merge_call(ya, yb, yc, p2d, x2d, wb, wo, ln):
    n = x2d.shape[0]
    tm = min(TM_DENSE, n)
    rowb = lambda i: (i, 0)
    cst = lambda i: (0, 0)
    gcol = lambda g: (lambda i: (i, g))
    return pl.pallas_call(
        _merge_body,
        grid=(n // tm,),
        in_specs=[
            pl.BlockSpec((tm, A_W), rowb), pl.BlockSpec((tm, B_W), rowb), pl.BlockSpec((tm, C_W), rowb),
            pl.BlockSpec((tm, D_MODEL), gcol(5)), pl.BlockSpec((tm, D_MODEL), gcol(6)),
            pl.BlockSpec((tm, D_MODEL), gcol(7)),
            pl.BlockSpec((tm, D_MODEL), rowb),
            pl.BlockSpec((A_W + B_W + C_W, D_MODEL), cst, pipeline_mode=pl.Buffered(1)),
            pl.BlockSpec((D_MODEL, D_MODEL), cst, pipeline_mode=pl.Buffered(1)),
            pl.BlockSpec((1, D_MODEL), cst),
        ],
        out_specs=pl.BlockSpec((tm, D_MODEL), rowb),
        out_shape=jax.ShapeDtypeStruct((n, D_MODEL), F32),
        compiler_params=pltpu.CompilerParams(
            dimension_semantics=("arbitrary",), vmem_limit_bytes=VMEM_LIMIT),
        name="merge",
    )(ya, yb, yc, p2d, p2d, p2d, x2d, wb, wo, ln)


def _mlp_body(x_ref, lnpre_ref, wu_ref, wd_ref, lnpost_ref, o_ref):
    x = x_ref[...]
    h = (x * lax.rsqrt(jnp.mean(x * x, axis=-1, keepdims=True) + EPS) * lnpre_ref[...]).astype(BF16)
    ff = jnp.zeros(x.shape, F32)
    for cidx in range(D_FF // D_MODEL):
        sl = slice(cidx * D_MODEL, (cidx + 1) * D_MODEL)
        u = jnp.maximum(_dot(h, wu_ref[:, sl]), 0.0)
        ff = ff + _dot((u * u).astype(BF16), wd_ref[sl, :])
    o_ref[...] = x + ff * lax.rsqrt(jnp.mean(ff * ff, axis=-1, keepdims=True) + EPS) * lnpost_ref[...]


def _mlp_call(x2d, lnpre, wu, wd, lnpost):
    n = x2d.shape[0]
    tm = min(TM_DENSE, n)
    rowb = lambda i: (i, 0)
    cst = lambda i: (0, 0)
    return pl.pallas_call(
        _mlp_body,
        grid=(n // tm,),
        in_specs=[
            pl.BlockSpec((tm, D_MODEL), rowb),
            pl.BlockSpec((1, D_MODEL), cst),
            pl.BlockSpec((D_MODEL, D_FF), cst, pipeline_mode=pl.Buffered(1)),
            pl.BlockSpec((D_FF, D_MODEL), cst, pipeline_mode=pl.Buffered(1)),
            pl.BlockSpec((1, D_MODEL), cst),
        ],
        out_specs=pl.BlockSpec((tm, D_MODEL), rowb),
        out_shape=jax.ShapeDtypeStruct((n, D_MODEL), F32),
        compiler_params=pltpu.CompilerParams(
            dimension_semantics=("arbitrary",), vmem_limit_bytes=VMEM_LIMIT),
        name="mlp",
    )(x2d, lnpre, wu, wd, lnpost)


def _rope_tables(seq):
    pos = jnp.arange(seq, dtype=F32)
    inv_a = ROPE_THETA ** (-jnp.arange(0, A_HEAD_DIM, 2, dtype=F32) / A_HEAD_DIM)
    ang_a = pos[:, None] * inv_a[None, :]
    cosa = jnp.concatenate([jnp.cos(ang_a)] * 2, axis=1)
    sina = jnp.concatenate([-jnp.sin(ang_a), jnp.sin(ang_a)], axis=1)
    inv_i = ROPE_THETA ** (-jnp.arange(0, IDX_DIM, 2, dtype=F32) / IDX_DIM)
    ang_i = pos[:, None] * inv_i[None, :]
    cosi = jnp.concatenate([jnp.cos(ang_i)] * 4, axis=1)
    sini = jnp.concatenate([-jnp.sin(ang_i), jnp.sin(ang_i)] * 2, axis=1)
    inv_c = 1.0 / (ROPE_THETA ** jnp.linspace(0.0, 1.0, C_KEY_DIM // 2, dtype=F32))
    ang_c = pos[:, None] * inv_c[None, :]
    cosc = jnp.concatenate([jnp.cos(ang_c)] * 4, axis=1)
    sinc = jnp.concatenate([-jnp.sin(ang_c), jnp.sin(ang_c)] * 2, axis=1)
    return (cosa, sina, cosi, sini), (cosc, sinc)


def _ret_tables():
    c = RET_BLK
    log_gamma = jnp.log(1.0 - 2.0 ** (-5.0 - jnp.arange(C_HEADS, dtype=F32)))
    r = jnp.arange(c, dtype=F32)
    rel = r[:, None] - r[None, :]
    dmat = jnp.where(rel >= 0, jnp.exp(jnp.maximum(rel, 0.0)[None] * log_gamma[:, None, None]), 0.0)
    xi = jnp.exp((r + 1.0)[None] * log_gamma[:, None])
    zeta = jnp.exp((c - 1.0 - r)[None] * log_gamma[:, None])
    gch = jnp.exp(c * log_gamma)
    xi_b = jnp.broadcast_to(xi[:, :, None], (C_HEADS, c, LANES))
    zeta_b = jnp.broadcast_to(zeta[:, :, None], (C_HEADS, c, LANES))
    gch_b = jnp.broadcast_to(gch[:, None, None], (C_HEADS, 8, LANES))
    return dmat, xi_b, zeta_b, gch_b


def _lane_vec(vals, ofs):
    v = jnp.zeros((1, LANES), F32)
    return v.at[0, ofs:ofs + vals.shape[0]].set(vals.astype(F32))


def kernel(x, ln_mix_pre, w_in, gdn_conv, gdn_a_log, gdn_dt_bias, gdn_norm, w_branch, w_out,
           ln_mix_post, ln_mlp_pre, w_up, w_down, ln_mlp_post):
    bsz, seq, _ = x.shape
    depth = w_in.shape[0]
    n = bsz * seq
    n_sel = min(TOPK_MAX, seq // 4)
    assert seq % BLK == 0 and n_sel % BLK == 0
    assert seq % (2 * KBLK) == 0
    dsa_tabs, (cosc, sinc) = _rope_tables(seq)
    dmat, xi_b, zeta_b, gch_b = _ret_tables()

    x2d = x.reshape(n, D_MODEL)
    for l in range(depth):
        w_big, w_small = _prep_w_in(w_in[l])
        p2d, sm2d = _proj_call(x2d, ln_mix_pre[l][None, :], w_big, w_small)
        p3 = p2d.reshape(bsz, seq, P_WIDTH)
        sm3 = sm2d.reshape(bsz, seq, LANES)
        y_a = _dsa_call(p3, sm3, dsa_tabs, n_sel)
        y_b = _gdn_call(p3, sm3, gdn_conv[l], _lane_vec(gdn_a_log[l], SM_BA),
                        _lane_vec(gdn_dt_bias[l], SM_BA), gdn_norm[l][None, :])
        y_c = _ret_call(p3, cosc, sinc, dmat, xi_b, zeta_b, gch_b)
        x2d = _merge_call(y_a.reshape(n, A_W), y_b.reshape(n, B_W), y_c.reshape(n, C_W), p2d, x2d,
                          w_branch[l].astype(BF16), w_out[l].astype(BF16), ln_mix_post[l][None, :])
        x2d = _mlp_call(x2d, ln_mlp_pre[l][None, :], w_up[l].astype(BF16), w_down[l].astype(BF16),
                        ln_mlp_post[l][None, :])
    return x2d.reshape(bsz, seq, D_MODEL)
```
